```python
import jax, jax.numpy as jnp
from jax import lax
import numpy as np

D_MODEL = 1024
BATCH = 2
SEQ = 16384
DEPTH = 4

D_MIX = D_MODEL
GROUP_W = D_MIX // 4
HEAD_DIM = 64
N_HEADS_G = GROUP_W // HEAD_DIM
CONV_W = 4
LRU_C = 8.0
DIL_PAIRS = ((128, 1), (512, 4), (2048, 16))
ROPE_THETA = 10000.0
DN_CHUNK = 64
N_EXPERTS = 16
EC_FACTOR = 2
D_FF_EXPERT = 2688
DN_ALPHA = (2.0 * DEPTH) ** 0.25
DN_BETA = (8.0 * DEPTH) ** -0.25
LN_EPS = 1e-5
RMS_EPS = 1e-6
NEG = -1e30

IN_SIZES = (GROUP_W,) * 10 + (2 * N_HEADS_G, 2 * N_HEADS_G)
D_IN = int(sum(IN_SIZES))
SPLIT_POINTS = tuple(int(v) for v in np.cumsum(IN_SIZES)[:-1])

kernel_name = 'hybrid_fourier_lru_dilattn_deltanet_ec_moe'


def layer_norm(x, g, b):
    xf = x.astype(jnp.float32)
    mu = jnp.mean(xf, -1, keepdims=True)
    var = jnp.mean(jnp.square(xf - mu), -1, keepdims=True)
    return ((xf - mu) * lax.rsqrt(var + LN_EPS) * g + b).astype(x.dtype)


def centred_dwconv(x, w, b):
    c = x.shape[-1]
    y = lax.conv_general_dilated(
        x, w[:, None, :].astype(x.dtype), window_strides=(1,),
        padding=[(CONV_W // 2, CONV_W - 1 - CONV_W // 2)],
        dimension_numbers=('NWC', 'WIO', 'NWC'), feature_group_count=c)
    return y + b.astype(x.dtype)


def fourier_mixer(xa, w_f, b_f):
    bsz, slen, _ = xa.shape
    xh = xa.astype(jnp.float32).reshape(bsz, slen, N_HEADS_G, HEAD_DIM)
    f = jnp.fft.fft2(xh, axes=(1, 3), norm='ortho').real
    y = jnp.einsum('bshc,hcd->bshd', f, w_f) + b_f
    return y.reshape(bsz, slen, GROUP_W)


def rglru_scan(xc, w_a, b_a, w_x, b_x, lam, reverse):
    bsz, slen, _ = xc.shape
    xh = xc.reshape(bsz, slen, N_HEADS_G, HEAD_DIM)
    r = jax.nn.sigmoid(jnp.einsum('bshc,hcd->bshd', xh, w_a).reshape(bsz, slen, GROUP_W) + b_a)
    i = jax.nn.sigmoid(jnp.einsum('bshc,hcd->bshd', xh, w_x).reshape(bsz, slen, GROUP_W) + b_x)
    log_a = -LRU_C * r * jax.nn.softplus(-lam)
    a = jnp.exp(log_a)
    u = jnp.sqrt(-jnp.expm1(2.0 * log_a)) * (i * xc)

    def combine(lhs, rhs):
        a1, u1 = lhs
        a2, u2 = rhs
        return a1 * a2, a2 * u1 + u2

    _, h = lax.associative_scan(combine, (a, u), reverse=reverse, axis=1)
    return h


def rglru_mixer(rec_in, rec_gate, conv_w, conv_b, w_a, b_a, w_x, b_x, lam):
    xc = centred_dwconv(rec_in.astype(jnp.float32), conv_w, conv_b).astype(jnp.float32)
    h = (rglru_scan(xc, w_a[0], b_a[0], w_x[0], b_x[0], lam[0], False)
         + rglru_scan(xc, w_a[1], b_a[1], w_x[1], b_x[1], lam[1], True))
    return h * jax.nn.gelu(rec_gate.astype(jnp.float32))


def rope(x, pos):
    half = HEAD_DIM // 2
    inv = ROPE_THETA ** (-jnp.arange(half, dtype=jnp.float32) / half)
    ang = pos.astype(jnp.float32)[..., None] * inv
    cos = jnp.cos(ang)[:, :, None, :]
    sin = jnp.sin(ang)[:, :, None, :]
    x1, x2 = x[..., :half], x[..., half:]
    return jnp.concatenate([x1 * cos - x2 * sin, x2 * cos + x1 * sin], axis=-1)


def dilated_window_attn(q, k, v, dil, radius):
    bsz, slen, nh, hd = q.shape
    sub_len = slen // dil
    nb = -(-sub_len // radius)
    sub_pad = nb * radius

    def to_blocks(t):
        t = t.reshape(bsz, sub_len, dil, nh, hd).transpose(0, 2, 1, 3, 4)
        t = jnp.pad(t, ((0, 0), (0, 0), (0, sub_pad - sub_len), (0, 0), (0, 0)))
        return t.reshape(bsz, dil, nb, radius, nh, hd)

    def neighbours(t):
        tp = jnp.pad(t, ((0, 0), (0, 0), (1, 1), (0, 0), (0, 0), (0, 0)))
        return jnp.concatenate([tp[:, :, :-2], tp[:, :, 1:-1], tp[:, :, 2:]], axis=3)

    qb = to_blocks(q)
    kn = neighbours(to_blocks(k))
    vn = neighbours(to_blocks(v))
    s = jnp.einsum('bgnqhe,bgnkhe->bgnhqk', qb, kn) * (hd ** -0.5)
    qi = jnp.arange(radius)[:, None]
    kj = jnp.arange(3 * radius)[None, :]
    band = jnp.abs(kj - radius - qi) <= radius
    kglob = (jnp.arange(nb)[:, None] - 1) * radius + jnp.arange(3 * radius)[None, :]
    inrange = (kglob >= 0) & (kglob < sub_len)
    mask = band[None, :, :] & inrange[:, None, :]
    s = jnp.where(mask[None, None, :, None], s, NEG)
    lse = jax.nn.logsumexp(s, axis=-1)
    p = jnp.exp(s - lse[..., None])
    o = jnp.einsum('bgnhqk,bgnkhe->bgnqhe', p, vn)
    o = o.reshape(bsz, dil, sub_pad, nh, hd)[:, :, :sub_len]
    o = o.transpose(0, 2, 1, 3, 4).reshape(bsz, slen, nh, hd)
    lse = lse.transpose(0, 1, 2, 4, 3).reshape(bsz, dil, sub_pad, nh)[:, :, :sub_len]
    lse = lse.transpose(0, 2, 1, 3).reshape(bsz, slen, nh)
    return o, lse


def dilated_attention_mixer(cq, ck, cv, positions):
    bsz, slen, _ = cq.shape

    def heads(t):
        return t.astype(jnp.float32).reshape(bsz, slen, N_HEADS_G, HEAD_DIM)

    q = rope(heads(cq), positions)
    k = rope(heads(ck), positions)
    v = heads(cv)
    outs, lses = [], []
    for window, dil in DIL_PAIRS:
        o, lse = dilated_window_attn(q, k, v, dil, (window // 2) // dil)
        outs.append(o)
        lses.append(lse)
    wts = jax.nn.softmax(jnp.stack(lses, axis=0), axis=0)
    o = jnp.einsum('gbsh,gbshe->bshe', wts, jnp.stack(outs, axis=0))
    return o.reshape(bsz, slen, GROUP_W)


def chunk_gated_delta(q, k, v, g, beta):
    bsz, slen, nh, dk = q.shape
    dv = v.shape[-1]
    cl = DN_CHUNK
    nc = slen // cl

    def chunks(t):
        t = t.reshape((bsz, nc, cl, nh) + t.shape[3:])
        return jnp.moveaxis(t, 3, 1)

    q = chunks(q) * (dk ** -0.5)
    k = chunks(k)
    v = chunks(v)
    g = jnp.cumsum(chunks(g), axis=-1)
    beta = chunks(beta)
    kb = k * beta[..., None]
    vb = v * beta[..., None]
    incl = jnp.tril(jnp.ones((cl, cl), dtype=bool))
    strict = jnp.tril(jnp.ones((cl, cl), dtype=bool), -1)
    diff = g[..., :, None] - g[..., None, :]
    decay = jnp.where(incl, jnp.exp(jnp.where(incl, diff, 0.0)), 0.0)
    m = jnp.einsum('bhnid,bhnjd->bhnij', kb, k) * jnp.where(strict, decay, 0.0)
    eye = jnp.eye(cl, dtype=q.dtype)
    tmat = lax.linalg.triangular_solve(eye + m, jnp.broadcast_to(eye, m.shape),
                                       left_side=True, lower=True)
    w = tmat @ vb
    u = tmat @ (kb * jnp.exp(g)[..., None])
    a_intra = jnp.einsum('bhnid,bhnjd->bhnij', q, k) * decay
    q_dec = q * jnp.exp(g)[..., None]
    g_last = g[..., -1]
    k_dec = k * jnp.exp(g_last[..., None] - g)[..., None]

    def step(state, xs):
        w_i, u_i, qd_i, kd_i, a_i, gl_i = xs
        v_new = w_i - u_i @ state
        o_i = qd_i @ state + a_i @ v_new
        state = state * jnp.exp(gl_i)[..., None, None] + jnp.einsum('bhck,bhcv->bhkv', kd_i, v_new)
        return state, o_i

    xs = tuple(jnp.moveaxis(t, 2, 0) for t in (w, u, q_dec, k_dec, a_intra, g_last))
    s0 = jnp.zeros((bsz, nh, dk, dv), q.dtype)
    _, o = lax.scan(step, s0, xs)
    o = jnp.moveaxis(o, 0, 2)
    return jnp.moveaxis(o, 1, 3).reshape(bsz, slen, nh, dv)


def l2norm(t):
    return t * lax.rsqrt(jnp.sum(t * t, axis=-1, keepdims=True) + RMS_EPS)


def gated_deltanet_mixer(dq, dk, dv, dg, dbeta, ddecay, conv_w, conv_b, a_log, dt_bias, norm_w):
    bsz, slen, _ = dq.shape
    f32 = jnp.float32
    qkv = jax.nn.silu(centred_dwconv(jnp.concatenate([dq, dk, dv], axis=-1).astype(f32), conv_w, conv_b))

    def heads(t):
        return t.astype(f32).reshape(bsz, slen, N_HEADS_G, HEAD_DIM)

    q, k, v = (heads(t) for t in jnp.split(qkv, 3, axis=-1))
    q = l2norm(q)
    k = l2norm(k)
    beta = jax.nn.sigmoid(dbeta.astype(f32))
    g = -jnp.exp(a_log.reshape(-1).astype(f32)) * jax.nn.softplus(
        ddecay.astype(f32) + dt_bias.reshape(-1))
    nh = N_HEADS_G

    def flip(t):
        return jnp.flip(t, axis=1)

    o_f = chunk_gated_delta(q, k, v, g[..., :nh], beta[..., :nh])
    o_b = flip(chunk_gated_delta(flip(q), flip(k), flip(v), flip(g[..., nh:]), flip(beta[..., nh:])))
    o = o_f + o_b
    o = o * lax.rsqrt(jnp.mean(o * o, axis=-1, keepdims=True) + RMS_EPS) * norm_w
    o = o * jax.nn.silu(heads(dg))
    return o.reshape(bsz, slen, GROUP_W)


def hybrid_mixer(x, positions, w_in, w_out, fno_w, fno_b, lru_conv_w, lru_conv_b, lru_wa, lru_ba,
                 lru_wx, lru_bx, lru_lam, dn_conv_w, dn_conv_b, dn_a_log, dn_dt_bias, dn_norm_w):
    parts = jnp.split(x @ w_in, SPLIT_POINTS, axis=-1)
    (a_x, rec_in, rec_gate, c_q, c_k, c_v, d_q, d_k, d_v, d_g, d_beta, d_decay) = parts
    y_a = fourier_mixer(a_x, fno_w, fno_b).astype(x.dtype)
    y_b = rglru_mixer(rec_in, rec_gate, lru_conv_w, lru_conv_b, lru_wa, lru_ba,
                      lru_wx, lru_bx, lru_lam).astype(x.dtype)
    y_c = dilated_attention_mixer(c_q, c_k, c_v, positions).astype(x.dtype)
    y_d = gated_deltanet_mixer(d_q, d_k, d_v, d_g, d_beta, d_decay, dn_conv_w, dn_conv_b,
                               dn_a_log, dn_dt_bias, dn_norm_w).astype(x.dtype)
    y = jnp.concatenate([y_a, y_b, y_c, y_d], axis=-1)
    return y @ w_out


def expert_choice_ffn(x, router_w, w1, w3, w2):
    bsz, slen, dm = x.shape
    cap = EC_FACTOR * slen // N_EXPERTS
    aff = jax.nn.softmax(jnp.einsum('bsd,de->bse', x, router_w).astype(jnp.float32), axis=-1)
    gate, idx = lax.top_k(jnp.swapaxes(aff, 1, 2), cap)
    idx_flat = idx.reshape(bsz, -1)
    xg = jax.vmap(lambda xb, ib: xb[ib])(x, idx_flat).reshape(bsz, N_EXPERTS, cap, dm)
    h = jax.nn.silu(jnp.einsum('becd,edf->becf', xg, w1)) * jnp.einsum('becd,edf->becf', xg, w3)
    y = jnp.einsum('becf,efd->becd', h, w2) * gate[..., None].astype(x.dtype)
    out = jax.vmap(lambda ib, ub: jnp.zeros((slen, dm), x.dtype).at[ib].add(ub))(
        idx_flat, y.reshape(bsz, -1, dm))
    return out


def setup_inputs(seed: int = 0) -> dict:
    key = jax.random.key(seed)
    ks = jax.random.split(key, 32)
    f32 = jnp.float32
    nh, hd = N_HEADS_G, HEAD_DIM

    def nrm(k, shape, scale):
        return jax.random.normal(k, shape, f32) * scale

    x = nrm(ks[0], (BATCH, SEQ, D_MODEL), 1.0)
    positions = (jax.random.randint(ks[1], (BATCH, 1), 0, 4096, dtype=jnp.int32)
                 + jnp.arange(SEQ, dtype=jnp.int32)[None, :])
    w_in = nrm(ks[2], (DEPTH, D_MODEL, D_IN), D_MODEL ** -0.5)
    w_out = nrm(ks[3], (DEPTH, D_MIX, D_MODEL), DN_BETA * D_MIX ** -0.5)
    fno_w = nrm(ks[4], (DEPTH, nh, hd, hd), hd ** -0.5)
    fno_b = nrm(ks[5], (DEPTH, nh, hd), 0.02)
    lru_conv_w = nrm(ks[6], (DEPTH, CONV_W, GROUP_W), CONV_W ** -0.5)
    lru_conv_b = nrm(ks[7], (DEPTH, GROUP_W), 0.02)
    lru_wa = nrm(ks[8], (DEPTH, 2, nh, hd, hd), hd ** -0.5)
    lru_ba = nrm(ks[9], (DEPTH, 2, GROUP_W), 0.02)
    lru_wx = nrm(ks[10], (DEPTH, 2, nh, hd, hd), hd ** -0.5)
    lru_bx = nrm(ks[11], (DEPTH, 2, GROUP_W), 0.02)
    a_c = jax.random.uniform(ks[12], (DEPTH, 2, GROUP_W), f32, 0.9, 0.999) ** (1.0 / LRU_C)
    lru_lam = jnp.log(a_c) - jnp.log1p(-a_c)
    dn_conv_w = nrm(ks[13], (DEPTH, CONV_W, 3 * GROUP_W), CONV_W ** -0.5)
    dn_conv_b = nrm(ks[14], (DEPTH, 3 * GROUP_W), 0.02)
    dn_a_log = jnp.log(jax.random.uniform(ks[15], (DEPTH, 2, nh), f32, 1.0, 16.0))
    dt = jnp.exp(jax.random.uniform(ks[16], (DEPTH, 2, nh), f32, float(np.log(1e-3)), float(np.log(1e-1))))
    dn_dt_bias = dt + jnp.log(-jnp.expm1(-dt))
    dn_norm_w = 1.0 + nrm(ks[17], (DEPTH, hd), 0.02)
    ln1_g = 1.0 + nrm(ks[18], (DEPTH, D_MODEL), 0.02)
    ln1_b = nrm(ks[19], (DEPTH, D_MODEL), 0.02)
    router_w = nrm(ks[20], (DEPTH, D_MODEL, N_EXPERTS), D_MODEL ** -0.5)
    exp_w1 = nrm(ks[21], (DEPTH, N_EXPERTS, D_MODEL, D_FF_EXPERT), D_MODEL ** -0.5)
    exp_w3 = nrm(ks[22], (DEPTH, N_EXPERTS, D_MODEL, D_FF_EXPERT), D_MODEL ** -0.5)
    exp_w2 = nrm(ks[23], (DEPTH, N_EXPERTS, D_FF_EXPERT, D_MODEL), DN_BETA * D_FF_EXPERT ** -0.5)
    ln2_g = 1.0 + nrm(ks[24], (DEPTH, D_MODEL), 0.02)
    ln2_b = nrm(ks[25], (DEPTH, D_MODEL), 0.02)
    return {'x': x, 'positions': positions, 'w_in': w_in, 'w_out': w_out,
            'fno_w': fno_w, 'fno_b': fno_b, 'lru_conv_w': lru_conv_w, 'lru_conv_b': lru_conv_b,
            'lru_wa': lru_wa, 'lru_ba': lru_ba, 'lru_wx': lru_wx, 'lru_bx': lru_bx,
            'lru_lam': lru_lam, 'dn_conv_w': dn_conv_w, 'dn_conv_b': dn_conv_b,
            'dn_a_log': dn_a_log, 'dn_dt_bias': dn_dt_bias, 'dn_norm_w': dn_norm_w,
            'ln1_g': ln1_g, 'ln1_b': ln1_b, 'router_w': router_w, 'exp_w1': exp_w1,
            'exp_w3': exp_w3, 'exp_w2': exp_w2, 'ln2_g': ln2_g, 'ln2_b': ln2_b}


def reference(x, positions, w_in, w_out, fno_w, fno_b, lru_conv_w, lru_conv_b, lru_wa, lru_ba,
              lru_wx, lru_bx, lru_lam, dn_conv_w, dn_conv_b, dn_a_log, dn_dt_bias, dn_norm_w,
              ln1_g, ln1_b, router_w, exp_w1, exp_w3, exp_w2, ln2_g, ln2_b):
    for l in range(DEPTH):
        mix = hybrid_mixer(x, positions, w_in[l], w_out[l], fno_w[l], fno_b[l],
                           lru_conv_w[l], lru_conv_b[l], lru_wa[l], lru_ba[l], lru_wx[l],
                           lru_bx[l], lru_lam[l], dn_conv_w[l], dn_conv_b[l], dn_a_log[l],
                           dn_dt_bias[l], dn_norm_w[l])
        x = layer_norm(DN_ALPHA * x + mix, ln1_g[l], ln1_b[l])
        moe = expert_choice_ffn(x, router_w[l], exp_w1[l], exp_w3[l], exp_w2[l])
        x = layer_norm(DN_ALPHA * x + moe, ln2_g[l], ln2_b[l])
    return x
```

```python
import functools
import math

import jax
import jax.numpy as jnp
from jax import lax
from jax.experimental import pallas as pl
from jax.experimental.pallas import tpu as pltpu

F32 = jnp.float32
BF16 = jnp.bfloat16
I32 = jnp.int32

GROUP_W = 256
HEAD_DIM = 64
N_HEADS_G = 4
LRU_C = 8.0
DIL_PAIRS = ((128, 1), (512, 4), (2048, 16))
ROPE_THETA = 10000.0
DN_CHUNK = 64
N_EXPERTS = 16
EC_FACTOR = 2
LN_EPS = 1e-5
RMS_EPS = 1e-6
NEG = -1e30

LANES = 128
SUBLANES = 8
VMEM_LIMIT = 56 * 1024 * 1024


def _cparams(sem):
    return pltpu.CompilerParams(dimension_semantics=sem, vmem_limit_bytes=VMEM_LIMIT)


def _dot(a, b):
    return jnp.dot(a, b, preferred_element_type=F32)


def _dot_nt(a, b):
    return lax.dot_general(a, b, (((1,), (1,)), ((), ())), preferred_element_type=F32)


def _dot_tn(a, b):
    return lax.dot_general(a, b, (((0,), (0,)), ((), ())), preferred_element_type=F32)


def _head_sum(t):
    lane = lax.broadcasted_iota(I32, t.shape, 1)
    head = lane // HEAD_DIM
    out = jnp.zeros_like(t)
    for h in range(N_HEADS_G):
        m = head == h
        s = jnp.sum(jnp.where(m, t, 0.0), axis=-1, keepdims=True)
        out = jnp.where(m, s, out)
    return out


def _layer_norm(v, g, b):
    mu = jnp.mean(v, axis=-1, keepdims=True)
    c = v - mu
    var = jnp.mean(c * c, axis=-1, keepdims=True)
    return c * lax.rsqrt(var + LN_EPS) * g + b


def _rope_table_kernel(pos_ref, inv_ref, sgn_ref, cos_ref, sin_ref):
    ang = pos_ref[...] * inv_ref[...]
    cos_ref[...] = jnp.cos(ang)
    sin_ref[...] = jnp.sin(ang) * sgn_ref[...]


def _rope_tables(positions):
    n = positions.size
    half = HEAD_DIM // 2
    inv = ROPE_THETA ** (-jnp.arange(half, dtype=F32) / half)
    inv = jnp.tile(inv, LANES // half)[None, :]
    lane = jnp.arange(LANES)
    sgn = jnp.where((lane % HEAD_DIM) < half, -1.0, 1.0).astype(F32)[None, :]
    pos = positions.reshape(n, 1).astype(F32)
    tm = min(2048, n)
    return pl.pallas_call(
        _rope_table_kernel,
        grid=(n // tm,),
        in_specs=[pl.BlockSpec((tm, 1), lambda i: (i, 0)),
                  pl.BlockSpec((1, LANES), lambda i: (0, 0)),
                  pl.BlockSpec((1, LANES), lambda i: (0, 0))],
        out_specs=[pl.BlockSpec((tm, LANES), lambda i: (i, 0))] * 2,
        out_shape=[jax.ShapeDtypeStruct((n, LANES), F32)] * 2,
        compiler_params=_cparams(("arbitrary",)),
        name="rope_tables",
    )(pos, inv, sgn)


def _inproj_kernel(x_ref, xp_ref, xn_ref, wm_ref, wc_ref, cw_ref, cb_ref, cos_ref, sin_ref,
                   nalog_ref, dtb_ref,
                   fa_ref, xc_ref, gg_ref, aq_ref, ak_ref, av_ref, dq_ref, dk_ref, dv_ref,
                   dsg_ref, dbg_ref, *, tiles_per_seq):
    i = pl.program_id(0)
    tm = x_ref.shape[0]
    gw = GROUP_W
    first = (i % tiles_per_seq) == 0
    last = (i % tiles_per_seq) == tiles_per_seq - 1
    xb = x_ref[...].astype(BF16)
    hm = _dot(xb, wm_ref[...])
    hc = _dot(xb, wc_ref[...])
    hp = _dot(xp_ref[...].astype(BF16), wc_ref[...]) * jnp.where(first, 0.0, 1.0)
    hn = _dot(xn_ref[...].astype(BF16), wc_ref[...]) * jnp.where(last, 0.0, 1.0)
    ext = jnp.concatenate([hp, hc, hn], axis=0)
    conv = cb_ref[...]
    for j in range(4):
        conv = conv + cw_ref[j:j + 1, :] * ext[6 + j:6 + j + tm, :]

    fa_ref[...] = hm[:, 0:gw]
    xc_ref[...] = conv[:, 0:gw]
    gg_ref[...] = jax.nn.gelu(hm[:, gw:2 * gw])

    cos2 = jnp.concatenate([cos_ref[...], cos_ref[...]], axis=1)
    sin2 = jnp.concatenate([sin_ref[...], sin_ref[...]], axis=1)
    lane = lax.broadcasted_iota(I32, (tm, gw), 1)
    lo = (lane % HEAD_DIM) < (HEAD_DIM // 2)

    def rope(t):
        rot = jnp.where(lo, pltpu.roll(t, gw - HEAD_DIM // 2, 1), pltpu.roll(t, HEAD_DIM // 2, 1))
        return t * cos2 + rot * sin2

    aq_ref[...] = (rope(hm[:, 2 * gw:3 * gw]) * (HEAD_DIM ** -0.5)).astype(BF16)
    ak_ref[...] = rope(hm[:, 3 * gw:4 * gw]).astype(BF16)
    av_ref[...] = hm[:, 4 * gw:5 * gw].astype(BF16)

    qkv = jax.nn.silu(conv[:, gw:])

    def l2n(t):
        return t * lax.rsqrt(_head_sum(t * t) + RMS_EPS)

    dq_ref[...] = l2n(qkv[:, 0:gw])
    dk_ref[...] = l2n(qkv[:, gw:2 * gw])
    dv_ref[...] = qkv[:, 2 * gw:3 * gw]
    dsg_ref[...] = jax.nn.silu(hm[:, 5 * gw:6 * gw])

    bd = hm[:, 6 * gw:6 * gw + LANES]
    l128 = lax.broadcasted_iota(I32, (tm, LANES), 1)
    beta = jax.nn.sigmoid(bd)
    g = nalog_ref[...] * jax.nn.softplus(bd + dtb_ref[...])
    nh2 = 2 * N_HEADS_G
    dbg_ref[...] = jnp.where(l128 < nh2, beta, jnp.where(l128 < 2 * nh2, g, 0.0))


def _inproj(x2d, cos_t, sin_t, wm, wc, cw, cb, nalog, dtb, seq_len):
    n, d = x2d.shape
    tm = min(512, seq_len)
    nt = n // tm
    tps = seq_len // tm
    hb = tm // SUBLANES
    gw = GROUP_W
    full = lambda shape: pl.BlockSpec(shape, lambda i: (0,) * len(shape))
    row = lambda w: pl.BlockSpec((tm, w), lambda i: (i, 0))
    out_w = [gw] * 10 + [LANES]
    out_dt = [F32, F32, F32, BF16, BF16, BF16, F32, F32, F32, F32, F32]
    return pl.pallas_call(
        functools.partial(_inproj_kernel, tiles_per_seq=tps),
        grid=(nt,),
        in_specs=[row(d),
                  pl.BlockSpec((SUBLANES, d), lambda i: (jnp.maximum(i * hb - 1, 0), 0)),
                  pl.BlockSpec((SUBLANES, d), lambda i: (jnp.minimum((i + 1) * hb, nt * hb - 1), 0)),
                  full(wm.shape), full(wc.shape), full(cw.shape), full(cb.shape),
                  row(LANES), row(LANES), full(nalog.shape), full(dtb.shape)],
        out_specs=[row(w) for w in out_w],
        out_shape=[jax.ShapeDtypeStruct((n, w), dt) for w, dt in zip(out_w, out_dt)],
        compiler_params=_cparams(("arbitrary",)),
        name="inproj",
    )(x2d, x2d, x2d, wm, wc, cw, cb, cos_t, sin_t, nalog, dtb)


def _fourier1_kernel(x_ref, t_ref, o_ref):
    for j in range(SUBLANES):
        xj = x_ref[:, j, :].astype(BF16)
        o_ref[j] = _dot(t_ref[j], xj)


def _fourier2_kernel(re_ref, im_ref, w2_ref, cc_ref, sc_ref, wf_ref, bf_ref, o_ref, *, scale):
    n2 = re_ref.shape[0]
    for j in range(SUBLANES):
        ar = re_ref[:, j, :].astype(BF16)
        ai = im_ref[:, j, :].astype(BF16)
        z = _dot(w2_ref[:, 0:n2], ar) + _dot(w2_ref[:, n2:2 * n2], ai)
        zr = z[0:n2].astype(BF16)
        zi = z[n2:2 * n2].astype(BF16)
        f = (_dot(zr, cc_ref[...]) + _dot(zi, sc_ref[...])) * scale
        o_ref[:, j, :] = _dot(f.astype(BF16), wf_ref[...]) + bf_ref[...]


def _dft_tables(seq_len):
    n1 = 1 << (int(math.log2(seq_len)) // 2)
    n2 = seq_len // n1
    k1 = jnp.arange(n1, dtype=I32)
    s1 = jnp.arange(n1, dtype=I32)
    s2 = jnp.arange(n2, dtype=I32)
    m = (k1[None, :, None] * (s1[None, None, :] * n2 + s2[:, None, None])) % seq_len
    ang = m.astype(F32) * (2.0 * math.pi / seq_len)
    t1 = jnp.concatenate([jnp.cos(ang), -jnp.sin(ang)], axis=1).astype(BF16)
    k2 = jnp.arange(n2, dtype=I32)
    m2 = (k2[:, None] * s2[None, :]) % n2
    a2 = m2.astype(F32) * (2.0 * math.pi / n2)
    c2, sn2 = jnp.cos(a2), jnp.sin(a2)
    w2 = jnp.concatenate([jnp.concatenate([c2, sn2], axis=1),
                          jnp.concatenate([-sn2, c2], axis=1)], axis=0).astype(BF16)
    c = jnp.arange(HEAD_DIM, dtype=I32)
    ac = ((c[:, None] * c[None, :]) % HEAD_DIM).astype(F32) * (2.0 * math.pi / HEAD_DIM)
    eye = jnp.eye(N_HEADS_G, dtype=F32)
    cc = jnp.kron(eye, jnp.cos(ac)).astype(BF16)
    sc = jnp.kron(eye, jnp.sin(ac)).astype(BF16)
    return n1, n2, t1, w2, cc, sc


def _block_diag(w):
    h, a, b = w.shape
    eye = jnp.eye(h, dtype=w.dtype)
    return (eye[:, None, :, None] * w[:, :, None, :]).reshape(h * a, h * b)


def _fourier(fa, tables, wf_bd, bf, batch, seq_len):
    n1, n2, t1, w2, cc, sc = tables
    gw = GROUP_W
    x4 = fa.reshape(batch, n1, n2, gw)
    a = pl.pallas_call(
        _fourier1_kernel,
        grid=(batch, n2 // SUBLANES),
        in_specs=[pl.BlockSpec((None, n1, SUBLANES, gw), lambda b, i: (b, 0, i, 0)),
                  pl.BlockSpec((SUBLANES, 2 * n1, n1), lambda b, i: (i, 0, 0))],
        out_specs=pl.BlockSpec((None, SUBLANES, 2 * n1, gw), lambda b, i: (b, i, 0, 0)),
        out_shape=jax.ShapeDtypeStruct((batch, n2, 2 * n1, gw), F32),
        compiler_params=_cparams(("arbitrary", "arbitrary")),
        name="fourier_stage1",
    )(x4, t1)
    nb = n1 // SUBLANES
    full = lambda arr: pl.BlockSpec(arr.shape, lambda b, i: (0,) * arr.ndim)
    y = pl.pallas_call(
        functools.partial(_fourier2_kernel, scale=1.0 / math.sqrt(seq_len * HEAD_DIM)),
        grid=(batch, nb),
        in_specs=[pl.BlockSpec((None, n2, SUBLANES, gw), lambda b, i: (b, 0, i, 0)),
                  pl.BlockSpec((None, n2, SUBLANES, gw), lambda b, i: (b, 0, nb + i, 0)),
                  full(w2), full(cc), full(sc), full(wf_bd), full(bf)],
        out_specs=pl.BlockSpec((None, n2, SUBLANES, gw), lambda b, i: (b, 0, i, 0)),
        out_shape=jax.ShapeDtypeStruct((batch, n2, n1, gw), F32),
        compiler_params=_cparams(("arbitrary", "arbitrary")),
        name="fourier_stage2",
    )(a, a, w2, cc, sc, wf_bd, bf)
    return y.reshape(batch * seq_len, gw)


def _lru_kernel(*refs, rev):
    if rev:
        xc_ref, w_ref, b_ref, lam_ref, hf_ref, gg_ref, o_ref, carry_ref = refs
    else:
        xc_ref, w_ref, b_ref, lam_ref, o_ref, carry_ref = refs
    gw = GROUP_W
    t = xc_ref.shape[0]

    @pl.when(pl.program_id(1) == 0)
    def _():
        carry_ref[...] = jnp.zeros_like(carry_ref)

    xc = xc_ref[...]
    gates = _dot(xc.astype(BF16), w_ref[...]) + b_ref[...]
    r = jax.nn.sigmoid(gates[:, 0:gw])
    ig = jax.nn.sigmoid(gates[:, gw:2 * gw])
    log_a = -LRU_C * r * jax.nn.softplus(-lam_ref[...])
    a = jnp.exp(log_a)
    th = jnp.tanh(log_a)
    u = jnp.sqrt(-2.0 * th / (1.0 - th)) * (ig * xc)

    r8 = lax.broadcasted_iota(I32, (t, gw), 0) % SUBLANES
    for s in (1, 2, 4):
        if rev:
            a_s, u_s, m = pltpu.roll(a, t - s, 0), pltpu.roll(u, t - s, 0), r8 < SUBLANES - s
        else:
            a_s, u_s, m = pltpu.roll(a, s, 0), pltpu.roll(u, s, 0), r8 >= s
        u = jnp.where(m, a * u_s + u, u)
        a = jnp.where(m, a * a_s, a)
    carry = carry_ref[...]
    ng = t // SUBLANES
    blocks = [None] * ng
    for g in (range(ng - 1, -1, -1) if rev else range(ng)):
        sl = slice(g * SUBLANES, (g + 1) * SUBLANES)
        blk = u[sl] + a[sl] * carry
        carry = blk[0:1] if rev else blk[SUBLANES - 1:SUBLANES]
        blocks[g] = blk
    carry_ref[...] = carry
    h = jnp.concatenate(blocks, axis=0)
    if rev:
        o_ref[...] = (hf_ref[...] + h) * gg_ref[...]
    else:
        o_ref[...] = h


def _lru(xc, gg, w_gate, b_gate, lam, batch, seq_len):
    n, gw = xc.shape
    t = min(256, seq_len)
    nt = seq_len // t
    fwd = lambda b, i: (b * nt + i, 0)
    bwd = lambda b, i: (b * nt + (nt - 1 - i), 0)
    outs = None
    for d, imap in ((0, fwd), (1, bwd)):
        par = lambda arr: pl.BlockSpec((None,) + arr.shape[1:], lambda b, i, d=d: (d,) + (0,) * (arr.ndim - 1))
        tile = pl.BlockSpec((t, gw), imap)
        ins = [xc, w_gate, b_gate, lam]
        specs = [tile, par(w_gate), par(b_gate), par(lam)]
        if d == 1:
            ins += [outs, gg]
            specs += [tile, tile]
        outs = pl.pallas_call(
            functools.partial(_lru_kernel, rev=bool(d)),
            grid=(batch, nt),
            in_specs=specs,
            out_specs=tile,
            out_shape=jax.ShapeDtypeStruct((n, gw), F32),
            scratch_shapes=[pltpu.VMEM((1, gw), F32)],
            compiler_params=_cparams(("arbitrary", "arbitrary")),
            name="lru_bwd" if d else "lru_fwd",
        )(*ins)
    return outs


ATT_R = 64
ATT_QB = 128


def _attn_kernel(q_ref, kp_ref, k_ref, kn_ref, vp_ref, v_ref, vn_ref, o_ref, l_ref, *, sub_len):
    tq = q_ref.shape[0]
    base = pl.program_id(2) * tq
    kext = jnp.concatenate([kp_ref[...], k_ref[...], kn_ref[...]], axis=0)
    vext = jnp.concatenate([vp_ref[...], v_ref[...], vn_ref[...]], axis=0)
    qb = min(ATT_QB, tq)
    kw = qb + 2 * ATT_R
    qi = lax.broadcasted_iota(I32, (qb, kw), 0)
    kj = lax.broadcasted_iota(I32, (qb, kw), 1)
    band = jnp.abs(kj - ATT_R - qi) <= ATT_R
    for jq in range(tq // qb):
        kpos = base + jq * qb + kj - ATT_R
        mask = band & (kpos >= 0) & (kpos < sub_len)
        o_parts, l_parts = [], []
        for h in range(N_HEADS_G):
            hs = slice(h * HEAD_DIM, (h + 1) * HEAD_DIM)
            q = q_ref[jq * qb:(jq + 1) * qb, hs]
            kk = kext[jq * qb:jq * qb + kw, hs]
            vv = vext[jq * qb:jq * qb + kw, hs]
            s = jnp.where(mask, _dot_nt(q, kk), NEG)
            m = jnp.max(s, axis=-1, keepdims=True)
            p = jnp.exp(s - m)
            l = jnp.sum(p, axis=-1, keepdims=True)
            o_parts.append(_dot(p.astype(BF16), vv) / l)
            l_parts.append(jnp.broadcast_to(m + jnp.log(l), (qb, HEAD_DIM)))
        o_ref[jq * qb:(jq + 1) * qb, :] = jnp.concatenate(o_parts, axis=1)
        l_ref[jq * qb:(jq + 1) * qb, :] = jnp.concatenate(l_parts, axis=1)


def _attn_pattern(aq, ak, av, dil, batch, seq_len):
    gw = GROUP_W
    sub = seq_len // dil
    q3, k3, v3 = (t.reshape(batch, sub, dil * gw) for t in (aq, ak, av))
    tq = min(512, sub)
    nq = sub // tq
    hb = tq // ATT_R
    nhb = sub // ATT_R
    main = pl.BlockSpec((None, tq, gw), lambda b, r, i: (b, i, r))
    prev = pl.BlockSpec((None, ATT_R, gw), lambda b, r, i: (b, jnp.maximum(i * hb - 1, 0), r))
    nxt = pl.BlockSpec((None, ATT_R, gw), lambda b, r, i: (b, jnp.minimum((i + 1) * hb, nhb - 1), r))
    o, l = pl.pallas_call(
        functools.partial(_attn_kernel, sub_len=sub),
        grid=(batch, dil, nq),
        in_specs=[main, prev, main, nxt, prev, main, nxt],
        out_specs=[main, main],
        out_shape=[jax.ShapeDtypeStruct((batch, sub, dil * gw), F32)] * 2,
        compiler_params=_cparams(("arbitrary", "arbitrary", "arbitrary")),
        name=f"dilated_attn_d{dil}",
    )(q3, k3, k3, k3, v3, v3, v3)
    n = batch * seq_len
    return o.reshape(n, gw), l.reshape(n, gw)


def _deltanet_kernel(qf_ref, kf_ref, vf_ref, gf_ref, qb_ref, kb_ref, vb_ref, gb_ref,
                     of_ref, ob_ref, state_ref, gt_ref, gc_ref):
    t = qf_ref.shape[0]
    cl = DN_CHUNK
    nc = t // cl
    nh = N_HEADS_G

    @pl.when(pl.program_id(1) == 0)
    def _():
        state_ref[...] = jnp.zeros_like(state_ref)

    rc = lax.broadcasted_iota(I32, (t, LANES), 0) % cl
    for d, g_ref in ((0, gf_ref), (1, gb_ref)):
        gc = g_ref[...]
        s = 1
        while s < cl:
            if d:
                gc = jnp.where(rc < cl - s, gc + pltpu.roll(gc, t - s, 0), gc)
            else:
                gc = jnp.where(rc >= s, gc + pltpu.roll(gc, s, 0), gc)
            s *= 2
        gc_ref[d] = gc
        gtr = gc.T
        for c in range(nc):
            gt_ref[d, c] = gtr[0:2 * SUBLANES, c * cl:(c + 1) * cl]

    ri = lax.broadcasted_iota(I32, (cl, cl), 0)
    ci = lax.broadcasted_iota(I32, (cl, cl), 1)
    eye = (ri == ci).astype(F32)

    def chunk_step(c, carry):
        for d in range(2):
            cc = (nc - 1 - c) if d else c
            r0 = pl.multiple_of(cc * cl, cl)
            q_ref, k_ref, v_ref, g_ref = (qb_ref, kb_ref, vb_ref, gb_ref) if d else (qf_ref, kf_ref, vf_ref, gf_ref)
            o_ref = ob_ref if d else of_ref
            q_all = q_ref[pl.ds(r0, cl), :]
            k_all = k_ref[pl.ds(r0, cl), :]
            v_all = v_ref[pl.ds(r0, cl), :]
            bg = g_ref[pl.ds(r0, cl), :]
            gc_all = gc_ref[d, pl.ds(r0, cl), :]
            gt = gt_ref[d, cc]
            incl = (ri <= ci) if d else (ri >= ci)
            strict = (ri < ci) if d else (ri > ci)
            outs = []
            for h in range(nh):
                hs = slice(h * HEAD_DIM, (h + 1) * HEAD_DIM)
                lb = d * nh + h
                lg = 2 * nh + d * nh + h
                q = q_all[:, hs] * (HEAD_DIM ** -0.5)
                k = k_all[:, hs]
                v = v_all[:, hs]
                beta = bg[:, lb:lb + 1]
                gcc = gc_all[:, lg:lg + 1]
                gcr = gt[lg:lg + 1, :]
                diff = gcc - gcr
                decay = jnp.where(incl, jnp.exp(jnp.where(incl, diff, 0.0)), 0.0)
                kb = k * beta
                vb = v * beta
                kbb = kb.astype(BF16)
                kbf = k.astype(BF16)
                m = _dot_nt(kbb, kbf) * jnp.where(strict, decay, 0.0)
                xp = -m
                tm = eye + xp
                for _ in range(5):
                    xb = xp.astype(BF16)
                    xp = _dot(xb, xb)
                    tm = tm + _dot(tm.astype(BF16), xp.astype(BF16))
                eg = jnp.exp(gcc)
                wu = _dot(tm.astype(BF16), jnp.concatenate([vb, kb * eg], axis=1).astype(BF16))
                w = wu[:, 0:HEAD_DIM]
                u = wu[:, HEAD_DIM:2 * HEAD_DIM]
                a_intra = _dot_nt(q.astype(BF16), kbf) * decay
                q_dec = q * eg
                gl = gcc[0:1, :] if d else gcc[cl - 1:cl, :]
                k_dec = k * jnp.exp(gl - gcc)
                st = state_ref[lb]
                stb = st.astype(BF16)
                v_new = w - _dot(u.astype(BF16), stb)
                vnb = v_new.astype(BF16)
                outs.append(_dot(q_dec.astype(BF16), stb) + _dot(a_intra.astype(BF16), vnb))
                state_ref[lb] = st * jnp.exp(gl) + _dot_tn(k_dec.astype(BF16), vnb)
            o_ref[pl.ds(r0, cl), :] = jnp.concatenate(outs, axis=1)
        return carry

    lax.fori_loop(0, nc, chunk_step, 0)


def _deltanet(dq, dk, dv, dbg, batch, seq_len):
    n, gw = dq.shape
    t = min(512, seq_len)
    nt = seq_len // t
    fwd = lambda b, i: (b * nt + i, 0)
    bwd = lambda b, i: (b * nt + (nt - 1 - i), 0)
    tf, tb = pl.BlockSpec((t, gw), fwd), pl.BlockSpec((t, gw), bwd)
    gf, gb = pl.BlockSpec((t, LANES), fwd), pl.BlockSpec((t, LANES), bwd)
    return pl.pallas_call(
        _deltanet_kernel,
        grid=(batch, nt),
        in_specs=[tf, tf, tf, gf, tb, tb, tb, gb],
        out_specs=[tf, tb],
        out_shape=[jax.ShapeDtypeStruct((n, gw), F32)] * 2,
        scratch_shapes=[pltpu.VMEM((2 * N_HEADS_G, HEAD_DIM, HEAD_DIM), F32),
                        pltpu.VMEM((2, t // DN_CHUNK, 2 * SUBLANES, DN_CHUNK), F32),
                        pltpu.VMEM((2, t, LANES), F32)],
        compiler_params=_cparams(("arbitrary", "arbitrary")),
        name="deltanet",
    )(dq, dk, dv, dbg, dq, dk, dv, dbg)


def _outproj_kernel(x_ref, fy_ref, yb_ref, o1_ref, l1_ref, o2_ref, l2_ref, o3_ref, l3_ref,
                    dof_ref, dob_ref, dsg_ref, wo_ref, nw_ref, g_ref, b_ref, rwt_ref,
                    x1_ref, aff_ref, *, alpha):
    gw = GROUP_W
    l1, l2, l3 = l1_ref[...], l2_ref[...], l3_ref[...]
    m = jnp.maximum(jnp.maximum(l1, l2), l3)
    w1, w2, w3 = jnp.exp(l1 - m), jnp.exp(l2 - m), jnp.exp(l3 - m)
    yc = (w1 * o1_ref[...] + w2 * o2_ref[...] + w3 * o3_ref[...]) / (w1 + w2 + w3)
    od = dof_ref[...] + dob_ref[...]
    ms = _head_sum(od * od) * (1.0 / HEAD_DIM)
    yd = od * lax.rsqrt(ms + RMS_EPS) * nw_ref[...] * dsg_ref[...]
    y = (_dot(fy_ref[...].astype(BF16), wo_ref[0:gw, :])
         + _dot(yb_ref[...].astype(BF16), wo_ref[gw:2 * gw, :])
         + _dot(yc.astype(BF16), wo_ref[2 * gw:3 * gw, :])
         + _dot(yd.astype(BF16), wo_ref[3 * gw:4 * gw, :]))
    x1 = _layer_norm(alpha * x_ref[...] + y, g_ref[...], b_ref[...])
    x1_ref[...] = x1
    logits = _dot_nt(rwt_ref[...], x1.astype(BF16))
    mx = jnp.max(logits, axis=0, keepdims=True)
    e = jnp.exp(logits - mx)
    aff_ref[...] = e / jnp.sum(e, axis=0, keepdims=True)


def _outproj(x2d, parts, wo, nw, g, b, rwt, batch, seq_len, alpha):
    n, d = x2d.shape
    gw = GROUP_W
    tm = min(512, seq_len)
    tps = seq_len // tm
    row = lambda w: pl.BlockSpec((tm, w), lambda i: (i, 0))
    full = lambda arr: pl.BlockSpec(arr.shape, lambda i: (0,) * arr.ndim)
    return pl.pallas_call(
        functools.partial(_outproj_kernel, alpha=alpha),
        grid=(n // tm,),
        in_specs=[row(d)] + [row(gw)] * 11 + [full(wo), full(nw), full(g), full(b), full(rwt)],
        out_specs=[row(d), pl.BlockSpec((None, N_EXPERTS, tm), lambda i: (i // tps, 0, i % tps))],
        out_shape=[jax.ShapeDtypeStruct((n, d), F32),
                   jax.ShapeDtypeStruct((batch, N_EXPERTS, seq_len), F32)],
        compiler_params=_cparams(("arbitrary",)),
        name="outproj_ln_router",
    )(x2d, *parts, wo, nw, g, b, rwt)


def _topk_kernel(a_ref, idx_ref, gate_ref, slot_ref, off_ref, *, cap):
    ng = a_ref.shape[0]
    v = a_ref[...]
    bits = pltpu.bitcast(v, I32)

    def bit_step(i, thr):
        cand = thr | (jnp.int32(1) << (30 - i))
        cnt = jnp.sum((bits >= cand).astype(I32))
        return jnp.where(cnt >= cap, cand, thr)

    thr = lax.fori_loop(0, 31, bit_step, jnp.int32(0))
    gt = bits > thr
    eq = bits == thr
    need_eq = cap - jnp.sum(gt.astype(I32))

    li = lax.broadcasted_iota(I32, (LANES, LANES), 0)
    lj = lax.broadcasted_iota(I32, (LANES, LANES), 1)
    ut_incl = (li <= lj).astype(BF16)
    gi = lax.broadcasted_iota(I32, (ng, ng), 0)
    gj = lax.broadcasted_iota(I32, (ng, ng), 1)
    lt_strict = (gj < gi).astype(BF16)

    def prefix(mask):
        p1 = _dot(mask.astype(BF16), ut_incl)
        tot = jnp.broadcast_to(p1[:, LANES - 1:LANES], (ng, LANES))
        return p1, _dot(lt_strict, tot.astype(BF16)), tot

    p1e, offe, _ = prefix(eq)
    rank_eq = p1e - eq.astype(F32) + offe
    sel = gt | (eq & (rank_eq < need_eq.astype(F32)))
    p1, offs, tot = prefix(sel)
    slot_ref[...] = jnp.where(sel, (p1 + offs).astype(I32) - 1, -1)
    off_ref[...] = offs.astype(I32)

    j = lax.broadcasted_iota(I32, (cap, 1), 0).astype(F32)
    ends_row = (offs + tot).T[0:1, :]
    offs_row = offs.T[0:1, :]
    gj_ = jnp.sum((ends_row <= j).astype(I32), axis=-1, keepdims=True)
    oh = lax.broadcasted_iota(I32, (cap, ng), 1) == gj_
    off_j = jnp.sum(jnp.where(oh, offs_row, 0.0), axis=-1, keepdims=True)
    ohb = oh.astype(BF16)
    prow = _dot(ohb, p1.astype(BF16))
    lo = jnp.sum((prow <= (j - off_j)).astype(I32), axis=-1, keepdims=True)
    idx_ref[...] = gj_ * LANES + lo
    h1 = v.astype(BF16)
    r1 = v - h1.astype(F32)
    h2 = r1.astype(BF16)
    h3 = (r1 - h2.astype(F32)).astype(BF16)
    arow = _dot(ohb, h1) + _dot(ohb, h2) + _dot(ohb, h3)
    lane = lax.broadcasted_iota(I32, (cap, LANES), 1)
    gate_ref[...] = jnp.sum(jnp.where(lane == lo, arow, 0.0), axis=-1, keepdims=True)


def _topk(aff_t, cap):
    batch, ne, seq_len = aff_t.shape
    ng = seq_len // LANES
    a4 = aff_t.reshape(batch, ne, ng, LANES)
    grp = pl.BlockSpec((None, None, ng, LANES), lambda b, e: (b, e, 0, 0))
    col = pl.BlockSpec((None, None, cap, 1), lambda b, e: (b, e, 0, 0))
    return pl.pallas_call(
        functools.partial(_topk_kernel, cap=cap),
        grid=(batch, ne),
        in_specs=[grp],
        out_specs=[col, col, grp, grp],
        out_shape=[jax.ShapeDtypeStruct((batch, ne, cap, 1), I32),
                   jax.ShapeDtypeStruct((batch, ne, cap, 1), F32),
                   jax.ShapeDtypeStruct((batch, ne, ng, LANES), I32),
                   jax.ShapeDtypeStruct((batch, ne, ng, LANES), I32)],
        compiler_params=_cparams(("arbitrary", "arbitrary")),
        name="expert_topk",
    )(a4)


FFN_ROWS = 512


def _ffn_kernel(idx_ref, x_hbm, gate_ref, w1_ref, w3_ref, w2_ref, y_ref, xg_ref, gbuf_ref, acc_ref, sem,
                *, seq_len):
    b = pl.program_id(0)
    rh = pl.program_id(2)
    f = pl.program_id(3)
    cap = xg_ref.shape[0]
    rb = min(FFN_ROWS, cap)

    @pl.when(f == 0)
    def _():
        for blk in range(cap // rb):
            def row_copy(r):
                tok = idx_ref[0, 0, rh * cap + blk * rb + r]
                return pltpu.make_async_copy(x_hbm.at[pl.ds(b * seq_len + tok, 1)],
                                             gbuf_ref.at[pl.ds(r, 1)], sem)

            def start(r, c):
                row_copy(r).start()
                return c

            def wait(r, c):
                row_copy(r).wait()
                return c

            lax.fori_loop(0, rb, start, 0)
            lax.fori_loop(0, rb, wait, 0)
            xg_ref[blk * rb:(blk + 1) * rb, :] = gbuf_ref[...].astype(BF16)

    w1 = w1_ref[...].astype(BF16)
    w3 = w3_ref[...].astype(BF16)
    w2 = w2_ref[...].astype(BF16)
    for blk in range(cap // rb):
        rs = slice(blk * rb, (blk + 1) * rb)
        xg = xg_ref[rs, :]
        h = (jax.nn.silu(_dot(xg, w1)) * _dot(xg, w3)).astype(BF16)
        part = _dot(h, w2)

        @pl.when(f == 0)
        def _():
            acc_ref[rs, :] = part

        @pl.when(f != 0)
        def _():
            acc_ref[rs, :] += part

    @pl.when(f == pl.num_programs(3) - 1)
    def _():
        y_ref[...] = (acc_ref[...] * gate_ref[...]).astype(BF16)


def _ffn(x1, idx, gate, w1, w3, w2, layer, batch, seq_len):
    n, d = x1.shape
    ne, cap = idx.shape[1], idx.shape[2]
    dff = w1.shape[-1]
    fc = 896 if dff % 896 == 0 else dff
    nf = dff // fc
    rows = min(1024, cap)
    nr = cap // rows
    idx3 = idx.reshape(batch * ne, 1, cap)
    return pl.pallas_call(
        functools.partial(_ffn_kernel, seq_len=seq_len),
        grid=(batch, ne, nr, nf),
        in_specs=[pl.BlockSpec((1, 1, cap), lambda b, e, r, f: (b * ne + e, 0, 0), memory_space=pltpu.SMEM),
                  pl.BlockSpec(memory_space=pl.ANY),
                  pl.BlockSpec((None, None, rows, 1), lambda b, e, r, f: (b, e, r, 0)),
                  pl.BlockSpec((None, None, d, fc), lambda b, e, r, f: (layer, e, 0, f)),
                  pl.BlockSpec((None, None, d, fc), lambda b, e, r, f: (layer, e, 0, f)),
                  pl.BlockSpec((None, None, fc, d), lambda b, e, r, f: (layer, e, f, 0))],
        out_specs=pl.BlockSpec((None, None, rows, d), lambda b, e, r, f: (b, e, r, 0)),
        out_shape=jax.ShapeDtypeStruct((batch, ne, cap, d), BF16),
        scratch_shapes=[pltpu.VMEM((rows, d), BF16),
                        pltpu.VMEM((min(FFN_ROWS, rows), d), F32),
                        pltpu.VMEM((rows, d), F32),
                        pltpu.SemaphoreType.DMA(())],
        compiler_params=_cparams(("arbitrary",) * 4),
        name="expert_ffn",
    )(idx3, x1, gate, w1, w3, w2)


CMB_WIN = 64
CMB_ALIGN = 16


def _combine_kernel(offs_ref, x_ref, slot_ref, y_hbm, g_ref, b_ref, o_ref, buf_ref, buf2_ref, acc_ref,
                    sem, sem2, *, alpha, cap, tiles_per_seq):
    i = pl.program_id(0)
    tc = x_ref.shape[0]
    ne = N_EXPERTS
    bidx = i // tiles_per_seq
    tidx = i % tiles_per_seq
    base = (bidx * (tiles_per_seq + 1) + tidx) * ne

    def window(e, k):
        off = offs_ref[base + e]
        start = (off // CMB_ALIGN) * CMB_ALIGN + k * CMB_WIN
        return pl.multiple_of(jnp.minimum(start, cap - CMB_WIN), CMB_ALIGN)

    def first_copy(e):
        return pltpu.make_async_copy(y_hbm.at[bidx, e, pl.ds(window(e, 0), CMB_WIN)], buf_ref.at[e], sem.at[e])

    for e in range(ne):
        first_copy(e).start()

    lane = lax.broadcasted_iota(I32, (tc, CMB_WIN), 1)
    slots = slot_ref[...]
    acc_ref[...] = alpha * x_ref[...]
    for e in range(ne):
        off = offs_ref[base + e]
        end = offs_ref[base + ne + e]
        slot = slots[:, e:e + 1]
        first_copy(e).wait()
        oh = (slot - window(e, 0)) == lane
        acc_ref[...] += _dot(oh.astype(BF16), buf_ref[e])
        n_win = (end - (off // CMB_ALIGN) * CMB_ALIGN + CMB_WIN - 1) // CMB_WIN

        def extra(k, c):
            w0 = window(e, k)
            cp = pltpu.make_async_copy(y_hbm.at[bidx, e, pl.ds(w0, CMB_WIN)], buf2_ref, sem2)
            cp.start()
            cp.wait()
            ohk = ((slot - w0) == lane) & ((slot - window(e, 0)) >= k * CMB_WIN)
            acc_ref[...] += _dot(ohk.astype(BF16), buf2_ref[...])
            return c

        lax.fori_loop(1, jnp.maximum(n_win, 1), extra, 0)
    o_ref[...] = _layer_norm(acc_ref[...], g_ref[...], b_ref[...])


def _combine(x1, y, slot_tm, offs, g, b, batch, seq_len, alpha):
    n, d = x1.shape
    ne, cap = y.shape[1], y.shape[2]
    tc = min(256, seq_len)
    tps = seq_len // tc
    return pl.pallas_call(
        functools.partial(_combine_kernel, alpha=alpha, cap=cap, tiles_per_seq=tps),
        grid_spec=pltpu.PrefetchScalarGridSpec(
            num_scalar_prefetch=1,
            grid=(n // tc,),
            in_specs=[pl.BlockSpec((tc, d), lambda i, o: (i, 0)),
                      pl.BlockSpec((tc, ne), lambda i, o: (i, 0)),
                      pl.BlockSpec(memory_space=pl.ANY),
                      pl.BlockSpec((1, d), lambda i, o: (0, 0)),
                      pl.BlockSpec((1, d), lambda i, o: (0, 0))],
            out_specs=pl.BlockSpec((tc, d), lambda i, o: (i, 0)),
            scratch_shapes=[pltpu.VMEM((ne, CMB_WIN, d), BF16),
                            pltpu.VMEM((CMB_WIN, d), BF16),
                            pltpu.VMEM((tc, d), F32),
                            pltpu.SemaphoreType.DMA((ne,)),
                            pltpu.SemaphoreType.DMA(())]),
        out_shape=jax.ShapeDtypeStruct((n, d), F32),
        compiler_params=_cparams(("arbitrary",)),
        name="combine_ln",
    )(offs, x1, slot_tm, y, g, b)


def _moe(x1, aff_t, w1, w3, w2, g2, b2, layer, batch, seq_len, alpha):
    cap = EC_FACTOR * seq_len // N_EXPERTS
    idx, gate, slot, offs = _topk(aff_t, cap)
    y = _ffn(x1, idx[..., 0], gate, w1, w3, w2, layer, batch, seq_len)
    tc = min(256, seq_len)
    slot_tm = jnp.transpose(slot.reshape(batch, N_EXPERTS, seq_len), (0, 2, 1)).reshape(batch * seq_len, N_EXPERTS)
    tile_off = offs[:, :, ::tc // LANES, 0]
    tile_off = jnp.concatenate([tile_off, jnp.full((batch, N_EXPERTS, 1), cap, I32)], axis=2)
    tile_off = jnp.transpose(tile_off, (0, 2, 1)).reshape(-1)
    return _combine(x1, y, slot_tm, tile_off, g2, b2, batch, seq_len, alpha)


def kernel(x, positions, w_in, w_out, fno_w, fno_b, lru_conv_w, lru_conv_b, lru_wa, lru_ba, lru_wx, lru_bx,
           lru_lam, dn_conv_w, dn_conv_b, dn_a_log, dn_dt_bias, dn_norm_w, ln1_g, ln1_b, router_w,
           exp_w1, exp_w3, exp_w2, ln2_g, ln2_b):
    batch, seq_len, d = x.shape
    depth = w_in.shape[0]
    gw = GROUP_W
    nh2 = 2 * N_HEADS_G
    alpha = (2.0 * depth) ** 0.25
    n = batch * seq_len

    cos_t, sin_t = _rope_tables(positions)
    tables = _dft_tables(seq_len)

    col = lambda a, b: w_in[:, :, a * gw:b * gw]
    bd = jnp.pad(w_in[:, :, 10 * gw:], ((0, 0), (0, 0), (0, LANES - 2 * nh2)))
    wm = jnp.concatenate([col(0, 1), col(2, 3), col(3, 6), col(9, 10), bd], axis=2).astype(BF16)
    wc = jnp.concatenate([col(1, 2), col(6, 9)], axis=2).astype(BF16)
    cw = jnp.concatenate([lru_conv_w, dn_conv_w], axis=2)
    cb = jnp.concatenate([lru_conv_b, dn_conv_b], axis=1)[:, None, :]
    pad16 = lambda v: jnp.pad(v.reshape(depth, 1, nh2), ((0, 0), (0, 0), (nh2, LANES - 2 * nh2)))
    nalog = pad16(-jnp.exp(dn_a_log.astype(F32)))
    dtb = pad16(dn_dt_bias)
    wf_bd = jax.vmap(_block_diag)(fno_w).astype(BF16)
    bf = fno_b.reshape(depth, 1, gw)
    lru_w = jnp.concatenate([jax.vmap(jax.vmap(_block_diag))(lru_wa),
                             jax.vmap(jax.vmap(_block_diag))(lru_wx)], axis=3).astype(BF16)
    lru_b = jnp.concatenate([lru_ba, lru_bx], axis=2)[:, :, None, :]
    lam = lru_lam[:, :, None, :]
    wo = w_out.astype(BF16)
    nw = jnp.tile(dn_norm_w, (1, N_HEADS_G))[:, None, :]
    rwt = jnp.transpose(router_w, (0, 2, 1)).astype(BF16)

    x2d = x.reshape(n, d)
    for l in range(depth):
        fa, xc, gg, aq, ak, av, dq, dk, dv, dsg, dbg = _inproj(
            x2d, cos_t, sin_t, wm[l], wc[l], cw[l], cb[l], nalog[l], dtb[l], seq_len)
        fy = _fourier(fa, tables, wf_bd[l], bf[l], batch, seq_len)
        yb = _lru(xc, gg, lru_w[l], lru_b[l], lam[l], batch, seq_len)
        att = []
        for window, dil in DIL_PAIRS:
            assert (window // 2) // dil == ATT_R
            att += list(_attn_pattern(aq, ak, av, dil, batch, seq_len))
        dof, dob = _deltanet(dq, dk, dv, dbg, batch, seq_len)
        x1, aff_t = _outproj(x2d, [fy, yb] + att + [dof, dob, dsg], wo[l], nw[l],
                             ln1_g[l][None, :], ln1_b[l][None, :], rwt[l], batch, seq_len, alpha)
        x2d = _moe(x1, aff_t, exp_w1, exp_w3, exp_w2, ln2_g[l][None, :], ln2_b[l][None, :],
                   l, batch, seq_len, alpha)
    return x2d.reshape(batch, seq_len, d)
```

```python
import functools
import math

import jax
import jax.numpy as jnp
from jax import lax
from jax.experimental import pallas as pl
from jax.experimental.pallas import tpu as pltpu

F32 = jnp.float32
BF16 = jnp.bfloat16
I32 = jnp.int32

GROUP_W = 256
HEAD_DIM = 64
N_HEADS_G = 4
LRU_C = 8.0
DIL_PAIRS = ((128, 1), (512, 4), (2048, 16))
ROPE_THETA = 10000.0
DN_CHUNK = 64
N_EXPERTS = 16
EC_FACTOR = 2
LN_EPS = 1e-5
RMS_EPS = 1e-6
NEG = -1e30

LANES = 128
SUBLANES = 8
VMEM_LIMIT = 56 * 1024 * 1024


def _cparams(sem):
    return pltpu.CompilerParams(dimension_semantics=sem, vmem_limit_bytes=VMEM_LIMIT)


def _dot(a, b):
    return jnp.dot(a, b, preferred_element_type=F32)


def _dot_nt(a, b):
    return lax.dot_general(a, b, (((1,), (1,)), ((), ())), preferred_element_type=F32)


def _head_sum(t):
    lane = lax.broadcasted_iota(I32, t.shape, 1)
    head = lane // HEAD_DIM
    out = jnp.zeros_like(t)
    for h in range(N_HEADS_G):
        m = head == h
        s = jnp.sum(jnp.where(m, t, 0.0), axis=-1, keepdims=True)
        out = jnp.where(m, s, out)
    return out


def _layer_norm(v, g, b):
    mu = jnp.mean(v, axis=-1, keepdims=True)
    c = v - mu
    var = jnp.mean(c * c, axis=-1, keepdims=True)
    return c * lax.rsqrt(var + LN_EPS) * g + b


def _rope_table_kernel(pos_ref, inv_ref, sgn_ref, cos_ref, sin_ref):
    ang = pos_ref[...] * inv_ref[...]
    cos_ref[...] = jnp.cos(ang)
    sin_ref[...] = jnp.sin(ang) * sgn_ref[...]


def _rope_tables(positions):
    n = positions.size
    half = HEAD_DIM // 2
    inv = ROPE_THETA ** (-jnp.arange(half, dtype=F32) / half)
    inv = jnp.tile(inv, LANES // half)[None, :]
    lane = jnp.arange(LANES)
    sgn = jnp.where((lane % HEAD_DIM) < half, -1.0, 1.0).astype(F32)[None, :]
    pos = positions.reshape(n, 1).astype(F32)
    tm = min(2048, n)
    return pl.pallas_call(
        _rope_table_kernel,
        grid=(n // tm,),
        in_specs=[pl.BlockSpec((tm, 1), lambda i: (i, 0)),
                  pl.BlockSpec((1, LANES), lambda i: (0, 0)),
                  pl.BlockSpec((1, LANES), lambda i: (0, 0))],
        out_specs=[pl.BlockSpec((tm, LANES), lambda i: (i, 0))] * 2,
        out_shape=[jax.ShapeDtypeStruct((n, LANES), F32)] * 2,
        compiler_params=_cparams(("arbitrary",)),
        name="rope_tables",
    )(pos, inv, sgn)


def _inproj_kernel(x_ref, xp_ref, xn_ref, wm_ref, wc_ref, cw_ref, cb_ref, cos_ref, sin_ref,
                   nalog_ref, dtb_ref,
                   fa_ref, xc_ref, gg_ref, *rest, tiles_per_seq):
    n_att = 3 * len(DIL_PAIRS)
    att_refs = rest[:n_att]
    dq_ref, dk_ref, dv_ref, dsg_ref, dbg_ref, scr_ref = rest[n_att:]
    i = pl.program_id(0)
    tm = x_ref.shape[0]
    gw = GROUP_W
    first = (i % tiles_per_seq) == 0
    last = (i % tiles_per_seq) == tiles_per_seq - 1
    xb = x_ref[...].astype(BF16)
    hm = _dot(xb, wm_ref[...])
    hc = _dot(xb, wc_ref[...])
    hp = _dot(xp_ref[...].astype(BF16), wc_ref[...]) * jnp.where(first, 0.0, 1.0)
    hn = _dot(xn_ref[...].astype(BF16), wc_ref[...]) * jnp.where(last, 0.0, 1.0)
    ext = jnp.concatenate([hp, hc, hn], axis=0)
    conv = cb_ref[...]
    for j in range(4):
        conv = conv + cw_ref[j:j + 1, :] * ext[6 + j:6 + j + tm, :]

    fa_ref[...] = hm[:, 0:gw]
    xc_ref[...] = conv[:, 0:gw]
    gg_ref[...] = jax.nn.gelu(hm[:, gw:2 * gw])

    cos2 = jnp.concatenate([cos_ref[...], cos_ref[...]], axis=1)
    sin2 = jnp.concatenate([sin_ref[...], sin_ref[...]], axis=1)
    lane = lax.broadcasted_iota(I32, (tm, gw), 1)
    lo = (lane % HEAD_DIM) < (HEAD_DIM // 2)

    def rope(t):
        rot = jnp.where(lo, pltpu.roll(t, gw - HEAD_DIM // 2, 1), pltpu.roll(t, HEAD_DIM // 2, 1))
        return t * cos2 + rot * sin2

    qkv_att = (rope(hm[:, 2 * gw:3 * gw]) * (HEAD_DIM ** -0.5), rope(hm[:, 3 * gw:4 * gw]), hm[:, 4 * gw:5 * gw])
    nl = gw // LANES
    for j, val in enumerate(qkv_att):
        for c in range(nl):
            scr_ref[j, c] = val[:, c * LANES:(c + 1) * LANES]
    for p, (_, dil) in enumerate(DIL_PAIRS):
        for j in range(3):
            ref = att_refs[3 * p + j]
            if dil == 1:
                ref[...] = qkv_att[j].astype(BF16)
            else:
                for r in range(dil):
                    for c in range(nl):
                        c0 = r * gw + c * LANES
                        ref[:, c0:c0 + LANES] = scr_ref[j, c, pl.ds(r, tm // dil, stride=dil), :].astype(BF16)

    qkv = jax.nn.silu(conv[:, gw:])

    def l2n(t):
        return t * lax.rsqrt(_head_sum(t * t) + RMS_EPS)

    dq_ref[...] = l2n(qkv[:, 0:gw])
    dk_ref[...] = l2n(qkv[:, gw:2 * gw])
    dv_ref[...] = qkv[:, 2 * gw:3 * gw]
    dsg_ref[...] = jax.nn.silu(hm[:, 5 * gw:6 * gw])

    bd = hm[:, 6 * gw:6 * gw + LANES]
    l128 = lax.broadcasted_iota(I32, (tm, LANES), 1)
    beta = jax.nn.sigmoid(bd)
    g = nalog_ref[...] * jax.nn.softplus(bd + dtb_ref[...])
    nh2 = 2 * N_HEADS_G
    dbg_ref[...] = jnp.where(l128 < nh2, beta, jnp.where(l128 < 2 * nh2, g, 0.0))


def _inproj(x2d, cos_t, sin_t, wm, wc, cw, cb, nalog, dtb, seq_len):
    n, d = x2d.shape
    tm = min(512, seq_len)
    nt = n // tm
    tps = seq_len // tm
    hb = tm // SUBLANES
    gw = GROUP_W
    full = lambda shape: pl.BlockSpec(shape, lambda i: (0,) * len(shape))
    row = lambda w: pl.BlockSpec((tm, w), lambda i: (i, 0))
    att_specs, att_shapes = [], []
    for _, dil in DIL_PAIRS:
        att_specs += [pl.BlockSpec((tm // dil, dil * gw), lambda i: (i, 0))] * 3
        att_shapes += [jax.ShapeDtypeStruct((n // dil, dil * gw), BF16)] * 3
    f32_out = lambda w: jax.ShapeDtypeStruct((n, w), F32)
    out_specs = [row(gw)] * 3 + att_specs + [row(gw)] * 4 + [row(LANES)]
    out_shape = [f32_out(gw)] * 3 + att_shapes + [f32_out(gw)] * 4 + [f32_out(LANES)]
    return pl.pallas_call(
        functools.partial(_inproj_kernel, tiles_per_seq=tps),
        grid=(nt,),
        in_specs=[row(d),
                  pl.BlockSpec((SUBLANES, d), lambda i: (jnp.maximum(i * hb - 1, 0), 0)),
                  pl.BlockSpec((SUBLANES, d), lambda i: (jnp.minimum((i + 1) * hb, nt * hb - 1), 0)),
                  full(wm.shape), full(wc.shape), full(cw.shape), full(cb.shape),
                  row(LANES), row(LANES), full(nalog.shape), full(dtb.shape)],
        out_specs=out_specs,
        out_shape=out_shape,
        scratch_shapes=[pltpu.VMEM((3, gw // LANES, tm, LANES), F32)],
        compiler_params=_cparams(("arbitrary",)),
        name="inproj",
    )(x2d, x2d, x2d, wm, wc, cw, cb, cos_t, sin_t, nalog, dtb)


def _fourier1_kernel(x_ref, t_ref, o_ref):
    for j in range(SUBLANES):
        xj = x_ref[:, j, :].astype(BF16)
        o_ref[j] = _dot(t_ref[j], xj)


def _fourier2_kernel(re_ref, im_ref, w2_ref, cc_ref, sc_ref, wf_ref, bf_ref, o_ref, *, scale):
    n2 = re_ref.shape[0]
    for j in range(SUBLANES):
        ar = re_ref[:, j, :].astype(BF16)
        ai = im_ref[:, j, :].astype(BF16)
        z = _dot(w2_ref[:, 0:n2], ar) + _dot(w2_ref[:, n2:2 * n2], ai)
        zr = z[0:n2].astype(BF16)
        zi = z[n2:2 * n2].astype(BF16)
        f = (_dot(zr, cc_ref[...]) + _dot(zi, sc_ref[...])) * scale
        o_ref[:, j, :] = _dot(f.astype(BF16), wf_ref[...]) + bf_ref[...]


def _dft_tables(seq_len):
    n1 = 1 << (int(math.log2(seq_len)) // 2)
    n2 = seq_len // n1
    k1 = jnp.arange(n1, dtype=I32)
    s1 = jnp.arange(n1, dtype=I32)
    s2 = jnp.arange(n2, dtype=I32)
    m = (k1[None, :, None] * (s1[None, None, :] * n2 + s2[:, None, None])) % seq_len
    ang = m.astype(F32) * (2.0 * math.pi / seq_len)
    t1 = jnp.concatenate([jnp.cos(ang), -jnp.sin(ang)], axis=1).astype(BF16)
    k2 = jnp.arange(n2, dtype=I32)
    m2 = (k2[:, None] * s2[None, :]) % n2
    a2 = m2.astype(F32) * (2.0 * math.pi / n2)
    c2, sn2 = jnp.cos(a2), jnp.sin(a2)
    w2 = jnp.concatenate([jnp.concatenate([c2, sn2], axis=1),
                          jnp.concatenate([-sn2, c2], axis=1)], axis=0).astype(BF16)
    c = jnp.arange(HEAD_DIM, dtype=I32)
    ac = ((c[:, None] * c[None, :]) % HEAD_DIM).astype(F32) * (2.0 * math.pi / HEAD_DIM)
    eye = jnp.eye(N_HEADS_G, dtype=F32)
    cc = jnp.kron(eye, jnp.cos(ac)).astype(BF16)
    sc = jnp.kron(eye, jnp.sin(ac)).astype(BF16)
    return n1, n2, t1, w2, cc, sc


def _block_diag(w):
    h, a, b = w.shape
    eye = jnp.eye(h, dtype=w.dtype)
    return (eye[:, None, :, None] * w[:, :, None, :]).reshape(h * a, h * b)


def _fourier(fa, tables, wf_bd, bf, batch, seq_len):
    n1, n2, t1, w2, cc, sc = tables
    gw = GROUP_W
    x4 = fa.reshape(batch, n1, n2, gw)
    a = pl.pallas_call(
        _fourier1_kernel,
        grid=(batch, n2 // SUBLANES),
        in_specs=[pl.BlockSpec((None, n1, SUBLANES, gw), lambda b, i: (b, 0, i, 0)),
                  pl.BlockSpec((SUBLANES, 2 * n1, n1), lambda b, i: (i, 0, 0))],
        out_specs=pl.BlockSpec((None, SUBLANES, 2 * n1, gw), lambda b, i: (b, i, 0, 0)),
        out_shape=jax.ShapeDtypeStruct((batch, n2, 2 * n1, gw), F32),
        compiler_params=_cparams(("arbitrary", "arbitrary")),
        name="fourier_stage1",
    )(x4, t1)
    nb = n1 // SUBLANES
    full = lambda arr: pl.BlockSpec(arr.shape, lambda b, i: (0,) * arr.ndim)
    y = pl.pallas_call(
        functools.partial(_fourier2_kernel, scale=1.0 / math.sqrt(seq_len * HEAD_DIM)),
        grid=(batch, nb),
        in_specs=[pl.BlockSpec((None, n2, SUBLANES, gw), lambda b, i: (b, 0, i, 0)),
                  pl.BlockSpec((None, n2, SUBLANES, gw), lambda b, i: (b, 0, nb + i, 0)),
                  full(w2), full(cc), full(sc), full(wf_bd), full(bf)],
        out_specs=pl.BlockSpec((None, n2, SUBLANES, gw), lambda b, i: (b, 0, i, 0)),
        out_shape=jax.ShapeDtypeStruct((batch, n2, n1, gw), F32),
        compiler_params=_cparams(("arbitrary", "arbitrary")),
        name="fourier_stage2",
    )(a, a, w2, cc, sc, wf_bd, bf)
    return y.reshape(batch * seq_len, gw)


def _lru_kernel(*refs, rev):
    if rev:
        xc_ref, w_ref, b_ref, lam_ref, hf_ref, gg_ref, o_ref, carry_ref = refs
    else:
        xc_ref, w_ref, b_ref, lam_ref, o_ref, carry_ref = refs
    gw = GROUP_W
    t = xc_ref.shape[0]

    @pl.when(pl.program_id(1) == 0)
    def _():
        carry_ref[...] = jnp.zeros_like(carry_ref)

    xc = xc_ref[...]
    gates = _dot(xc.astype(BF16), w_ref[...]) + b_ref[...]
    r = jax.nn.sigmoid(gates[:, 0:gw])
    ig = jax.nn.sigmoid(gates[:, gw:2 * gw])
    log_a = -LRU_C * r * jax.nn.softplus(-lam_ref[...])
    a = jnp.exp(log_a)
    th = jnp.tanh(log_a)
    u = jnp.sqrt(-2.0 * th / (1.0 - th)) * (ig * xc)

    r8 = lax.broadcasted_iota(I32, (t, gw), 0) % SUBLANES
    for s in (1, 2, 4):
        if rev:
            a_s, u_s, m = pltpu.roll(a, t - s, 0), pltpu.roll(u, t - s, 0), r8 < SUBLANES - s
        else:
            a_s, u_s, m = pltpu.roll(a, s, 0), pltpu.roll(u, s, 0), r8 >= s
        u = jnp.where(m, a * u_s + u, u)
        a = jnp.where(m, a * a_s, a)
    carry = carry_ref[...]
    ng = t // SUBLANES
    blocks = [None] * ng
    for g in (range(ng - 1, -1, -1) if rev else range(ng)):
        sl = slice(g * SUBLANES, (g + 1) * SUBLANES)
        blk = u[sl] + a[sl] * carry
        carry = blk[0:1] if rev else blk[SUBLANES - 1:SUBLANES]
        blocks[g] = blk
    carry_ref[...] = carry
    h = jnp.concatenate(blocks, axis=0)
    if rev:
        o_ref[...] = (hf_ref[...] + h) * gg_ref[...]
    else:
        o_ref[...] = h


def _lru(xc, gg, w_gate, b_gate, lam, batch, seq_len):
    n, gw = xc.shape
    t = min(256, seq_len)
    nt = seq_len // t
    fwd = lambda b, i: (b * nt + i, 0)
    bwd = lambda b, i: (b * nt + (nt - 1 - i), 0)
    outs = None
    for d, imap in ((0, fwd), (1, bwd)):
        par = lambda arr: pl.BlockSpec((None,) + arr.shape[1:], lambda b, i, d=d: (d,) + (0,) * (arr.ndim - 1))
        tile = pl.BlockSpec((t, gw), imap)
        ins = [xc, w_gate, b_gate, lam]
        specs = [tile, par(w_gate), par(b_gate), par(lam)]
        if d == 1:
            ins += [outs, gg]
            specs += [tile, tile]
        outs = pl.pallas_call(
            functools.partial(_lru_kernel, rev=bool(d)),
            grid=(batch, nt),
            in_specs=specs,
            out_specs=tile,
            out_shape=jax.ShapeDtypeStruct((n, gw), F32),
            scratch_shapes=[pltpu.VMEM((1, gw), F32)],
            compiler_params=_cparams(("arbitrary", "arbitrary")),
            name="lru_bwd" if d else "lru_fwd",
        )(*ins)
    return outs


ATT_R = 64
ATT_QB = 128


def _attn_kernel(q_ref, kp_ref, k_ref, kn_ref, vp_ref, v_ref, vn_ref, o_ref, l_ref, *, sub_len):
    tq = q_ref.shape[0]
    base = pl.program_id(2) * tq
    kext = jnp.concatenate([kp_ref[...], k_ref[...], kn_ref[...]], axis=0)
    vext = jnp.concatenate([vp_ref[...], v_ref[...], vn_ref[...]], axis=0)
    qb = min(ATT_QB, tq)
    kw = qb + 2 * ATT_R
    qi = lax.broadcasted_iota(I32, (qb, kw), 0)
    kj = lax.broadcasted_iota(I32, (qb, kw), 1)
    band = jnp.abs(kj - ATT_R - qi) <= ATT_R
    for jq in range(tq // qb):
        kpos = base + jq * qb + kj - ATT_R
        mask = band & (kpos >= 0) & (kpos < sub_len)
        o_parts, l_parts = [], []
        for h in range(N_HEADS_G):
            hs = slice(h * HEAD_DIM, (h + 1) * HEAD_DIM)
            q = q_ref[jq * qb:(jq + 1) * qb, hs]
            kk = kext[jq * qb:jq * qb + kw, hs]
            vv = vext[jq * qb:jq * qb + kw, hs]
            s = jnp.where(mask, _dot_nt(q, kk), NEG)
            m = jnp.max(s, axis=-1, keepdims=True)
            p = jnp.exp(s - m)
            l = jnp.sum(p, axis=-1, keepdims=True)
            o_parts.append(_dot(p.astype(BF16), vv) / l)
            l_parts.append(jnp.broadcast_to(m + jnp.log(l), (qb, HEAD_DIM)))
        o_ref[jq * qb:(jq + 1) * qb, :] = jnp.concatenate(o_parts, axis=1)
        l_ref[jq * qb:(jq + 1) * qb, :] = jnp.concatenate(l_parts, axis=1)


def _attn_pattern(aq, ak, av, dil, batch, seq_len):
    gw = GROUP_W
    sub = seq_len // dil
    q3, k3, v3 = (t.reshape(batch, sub, dil * gw) for t in (aq, ak, av))
    tq = min(512, sub)
    nq = sub // tq
    hb = tq // ATT_R
    nhb = sub // ATT_R
    main = pl.BlockSpec((None, tq, gw), lambda b, r, i: (b, i, r))
    prev = pl.BlockSpec((None, ATT_R, gw), lambda b, r, i: (b, jnp.maximum(i * hb - 1, 0), r))
    nxt = pl.BlockSpec((None, ATT_R, gw), lambda b, r, i: (b, jnp.minimum((i + 1) * hb, nhb - 1), r))
    o, l = pl.pallas_call(
        functools.partial(_attn_kernel, sub_len=sub),
        grid=(batch, dil, nq),
        in_specs=[main, prev, main, nxt, prev, main, nxt],
        out_specs=[main, main],
        out_shape=[jax.ShapeDtypeStruct((batch, sub, dil * gw), F32)] * 2,
        compiler_params=_cparams(("arbitrary", "arbitrary", "arbitrary")),
        name=f"dilated_attn_d{dil}",
    )(q3, k3, k3, k3, v3, v3, v3)
    return o.reshape(batch * sub, dil * gw), l.reshape(batch * sub, dil * gw)


DN_TILE = 256


def _bmm(a, b):
    return jnp.einsum('bij,bjk->bik', a, b, preferred_element_type=F32)


def _bmm_nt(a, b):
    return jnp.einsum('bij,bkj->bik', a, b, preferred_element_type=F32)


def _deltanet_kernel(qf_ref, kf_ref, vf_ref, gf_ref, qb_ref, kb_ref, vb_ref, gb_ref,
                     of_ref, ob_ref, state_ref):
    t = qf_ref.shape[0]
    cl = DN_CHUNK
    nc = t // cl
    nh = N_HEADS_G
    hd = HEAD_DIM
    nb = 2 * nc * nh

    @pl.when(pl.program_id(1) == 0)
    def _():
        state_ref[...] = jnp.zeros_like(state_ref)

    rc = lax.broadcasted_iota(I32, (t, LANES), 0) % cl
    gcs, gts = [], []
    for d, g_ref in ((0, gf_ref), (1, gb_ref)):
        gc = g_ref[...]
        s = 1
        while s < cl:
            if d:
                gc = jnp.where(rc < cl - s, gc + pltpu.roll(gc, t - s, 0), gc)
            else:
                gc = jnp.where(rc >= s, gc + pltpu.roll(gc, s, 0), gc)
            s *= 2
        gcs.append(gc)
        gts.append(gc.T)

    order = [(d, c, h) for d in range(2) for c in range(nc) for h in range(nh)]
    data = ((qf_ref, kf_ref, vf_ref, gf_ref), (qb_ref, kb_ref, vb_ref, gb_ref))

    def pick(which):
        return jnp.stack([data[d][which][c * cl:(c + 1) * cl, h * hd:(h + 1) * hd] for d, c, h in order])

    q3, k3, v3 = pick(0), pick(1), pick(2)
    beta3 = jnp.stack([data[d][3][c * cl:(c + 1) * cl, d * nh + h:d * nh + h + 1] for d, c, h in order])
    gcc3 = jnp.stack([gcs[d][c * cl:(c + 1) * cl, 2 * nh + d * nh + h:2 * nh + d * nh + h + 1]
                      for d, c, h in order])
    gcr3 = jnp.stack([gts[d][2 * nh + d * nh + h:2 * nh + d * nh + h + 1, c * cl:(c + 1) * cl]
                      for d, c, h in order])

    ri = lax.broadcasted_iota(I32, (nb, cl, cl), 1)
    ci = lax.broadcasted_iota(I32, (nb, cl, cl), 2)
    isb = lax.broadcasted_iota(I32, (nb, cl, cl), 0) >= nc * nh
    r2 = jnp.where(isb, ci, ri)
    c2 = jnp.where(isb, ri, ci)
    incl = r2 >= c2
    strict = r2 > c2
    eye = (ri == ci).astype(F32)

    decay = jnp.where(incl, jnp.exp(jnp.where(incl, gcc3 - gcr3, 0.0)), 0.0)
    kb3 = k3 * beta3
    vb3 = v3 * beta3
    kbf = k3.astype(BF16)
    m = _bmm_nt(kb3.astype(BF16), kbf) * jnp.where(strict, decay, 0.0)
    xp = -m
    tm = eye + xp
    for _ in range(5):
        xb = xp.astype(BF16)
        xp = _bmm(xb, xb)
        tm = tm + _bmm(tm.astype(BF16), xp.astype(BF16))
    eg = jnp.exp(gcc3)
    wu = _bmm(tm.astype(BF16), jnp.concatenate([vb3, kb3 * eg], axis=2).astype(BF16))
    w3 = wu[:, :, 0:hd]
    u3 = wu[:, :, hd:2 * hd].astype(BF16)
    qs = q3 * (hd ** -0.5)
    a3 = (_bmm_nt(qs.astype(BF16), kbf) * decay).astype(BF16)
    qd3 = (qs * eg).astype(BF16)
    gl3 = jnp.where(isb[:, 0:1, 0:1], gcc3[:, 0:1, :], gcc3[:, cl - 1:cl, :])
    kd3 = (k3 * jnp.exp(gl3 - gcc3)).astype(BF16)
    kdt3 = _bmm_nt(eye.astype(BF16), kd3).astype(BF16)
    egl3 = jnp.exp(gl3)

    def step_bodies(arr, c):
        f0 = c * nh
        b0 = nc * nh + (nc - 1 - c) * nh
        return jnp.concatenate([arr[f0:f0 + nh], arr[b0:b0 + nh]], axis=0)

    st = state_ref[...]
    for c in range(nc):
        stb = st.astype(BF16)
        v_new = step_bodies(w3, c) - _bmm(step_bodies(u3, c), stb)
        vnb = v_new.astype(BF16)
        o8 = _bmm(step_bodies(qd3, c), stb) + _bmm(step_bodies(a3, c), vnb)
        st = st * step_bodies(egl3, c) + _bmm(step_bodies(kdt3, c), vnb)
        cb = nc - 1 - c
        of_ref[c * cl:(c + 1) * cl, :] = jnp.concatenate([o8[h] for h in range(nh)], axis=1)
        ob_ref[cb * cl:(cb + 1) * cl, :] = jnp.concatenate([o8[nh + h] for h in range(nh)], axis=1)
    state_ref[...] = st


def _deltanet(dq, dk, dv, dbg, batch, seq_len):
    n, gw = dq.shape
    t = min(DN_TILE, seq_len)
    nt = seq_len // t
    fwd = lambda b, i: (b * nt + i, 0)
    bwd = lambda b, i: (b * nt + (nt - 1 - i), 0)
    tf, tb = pl.BlockSpec((t, gw), fwd), pl.BlockSpec((t, gw), bwd)
    gf, gb = pl.BlockSpec((t, LANES), fwd), pl.BlockSpec((t, LANES), bwd)
    return pl.pallas_call(
        _deltanet_kernel,
        grid=(batch, nt),
        in_specs=[tf, tf, tf, gf, tb, tb, tb, gb],
        out_specs=[tf, tb],
        out_shape=[jax.ShapeDtypeStruct((n, gw), F32)] * 2,
        scratch_shapes=[pltpu.VMEM((2 * N_HEADS_G, HEAD_DIM, HEAD_DIM), F32)],
        compiler_params=_cparams(("arbitrary", "arbitrary")),
        name="deltanet",
    )(dq, dk, dv, dbg, dq, dk, dv, dbg)


def _outproj_kernel(x_ref, fy_ref, yb_ref, o1_ref, l1_ref, o2_ref, l2_ref, o3_ref, l3_ref,
                    dof_ref, dob_ref, dsg_ref, wo_ref, nw_ref, g_ref, b_ref, rwt_ref,
                    x1_ref, aff_ref, scr_ref, *, alpha):
    gw = GROUP_W
    tm = x_ref.shape[0]
    att = []
    for p, ((_, dil), refs) in enumerate(zip(DIL_PAIRS, ((o1_ref, l1_ref), (o2_ref, l2_ref), (o3_ref, l3_ref)))):
        for j, ref in enumerate(refs):
            if dil == 1:
                att.append(ref[...])
            else:
                nl = gw // LANES
                for r in range(dil):
                    for c in range(nl):
                        c0 = r * gw + c * LANES
                        scr_ref[2 * p + j, c, pl.ds(r, tm // dil, stride=dil), :] = ref[:, c0:c0 + LANES]
                att.append(jnp.concatenate([scr_ref[2 * p + j, c] for c in range(nl)], axis=1))
    o1, l1, o2, l2, o3, l3 = att
    m = jnp.maximum(jnp.maximum(l1, l2), l3)
    w1, w2, w3 = jnp.exp(l1 - m), jnp.exp(l2 - m), jnp.exp(l3 - m)
    yc = (w1 * o1 + w2 * o2 + w3 * o3) / (w1 + w2 + w3)
    od = dof_ref[...] + dob_ref[...]
    ms = _head_sum(od * od) * (1.0 / HEAD_DIM)
    yd = od * lax.rsqrt(ms + RMS_EPS) * nw_ref[...] * dsg_ref[...]
    y = (_dot(fy_ref[...].astype(BF16), wo_ref[0:gw, :])
         + _dot(yb_ref[...].astype(BF16), wo_ref[gw:2 * gw, :])
         + _dot(yc.astype(BF16), wo_ref[2 * gw:3 * gw, :])
         + _dot(yd.astype(BF16), wo_ref[3 * gw:4 * gw, :]))
    x1 = _layer_norm(alpha * x_ref[...] + y, g_ref[...], b_ref[...])
    x1_ref[...] = x1
    logits = _dot_nt(rwt_ref[...], x1.astype(BF16))
    mx = jnp.max(logits, axis=0, keepdims=True)
    e = jnp.exp(logits - mx)
    aff_ref[...] = e / jnp.sum(e, axis=0, keepdims=True)


def _outproj(x2d, parts, wo, nw, g, b, rwt, batch, seq_len, alpha):
    n, d = x2d.shape
    gw = GROUP_W
    tm = min(512, seq_len)
    tps = seq_len // tm
    row = lambda w: pl.BlockSpec((tm, w), lambda i: (i, 0))
    full = lambda arr: pl.BlockSpec(arr.shape, lambda i: (0,) * arr.ndim)
    att_specs = []
    for _, dil in DIL_PAIRS:
        att_specs += [pl.BlockSpec((tm // dil, dil * gw), lambda i: (i, 0))] * 2
    return pl.pallas_call(
        functools.partial(_outproj_kernel, alpha=alpha),
        grid=(n // tm,),
        in_specs=[row(d)] + [row(gw)] * 2 + att_specs + [row(gw)] * 3
                 + [full(wo), full(nw), full(g), full(b), full(rwt)],
        out_specs=[row(d), pl.BlockSpec((None, N_EXPERTS, tm), lambda i: (i // tps, 0, i % tps))],
        out_shape=[jax.ShapeDtypeStruct((n, d), F32),
                   jax.ShapeDtypeStruct((batch, N_EXPERTS, seq_len), F32)],
        scratch_shapes=[pltpu.VMEM((2 * len(DIL_PAIRS), gw // LANES, tm, LANES), F32)],
        compiler_params=_cparams(("arbitrary",)),
        name="outproj_ln_router",
    )(x2d, *parts, wo, nw, g, b, rwt)


def _topk_kernel(a_ref, idx_ref, gate_ref, slot_ref, off_ref, *, cap):
    ng = a_ref.shape[0]
    v = a_ref[...]
    bits = pltpu.bitcast(v, I32)

    def bit_step(i, thr):
        cand = thr | (jnp.int32(1) << (30 - i))
        cnt = jnp.sum((bits >= cand).astype(I32))
        return jnp.where(cnt >= cap, cand, thr)

    thr = lax.fori_loop(0, 31, bit_step, jnp.int32(0))
    gt = bits > thr
    eq = bits == thr
    need_eq = cap - jnp.sum(gt.astype(I32))

    li = lax.broadcasted_iota(I32, (LANES, LANES), 0)
    lj = lax.broadcasted_iota(I32, (LANES, LANES), 1)
    ut_incl = (li <= lj).astype(BF16)
    gi = lax.broadcasted_iota(I32, (ng, ng), 0)
    gj = lax.broadcasted_iota(I32, (ng, ng), 1)
    lt_strict = (gj < gi).astype(BF16)

    def prefix(mask):
        p1 = _dot(mask.astype(BF16), ut_incl)
        tot = jnp.broadcast_to(p1[:, LANES - 1:LANES], (ng, LANES))
        return p1, _dot(lt_strict, tot.astype(BF16)), tot

    p1e, offe, _ = prefix(eq)
    rank_eq = p1e - eq.astype(F32) + offe
    sel = gt | (eq & (rank_eq < need_eq.astype(F32)))
    p1, offs, tot = prefix(sel)
    slot_ref[...] = jnp.where(sel, (p1 + offs).astype(I32) - 1, -1)
    off_ref[...] = offs.astype(I32)

    j = lax.broadcasted_iota(I32, (cap, 1), 0).astype(F32)
    ends_row = (offs + tot).T[0:1, :]
    offs_row = offs.T[0:1, :]
    gj_ = jnp.sum((ends_row <= j).astype(I32), axis=-1, keepdims=True)
    oh = lax.broadcasted_iota(I32, (cap, ng), 1) == gj_
    off_j = jnp.sum(jnp.where(oh, offs_row, 0.0), axis=-1, keepdims=True)
    ohb = oh.astype(BF16)
    prow = _dot(ohb, p1.astype(BF16))
    lo = jnp.sum((prow <= (j - off_j)).astype(I32), axis=-1, keepdims=True)
    idx_ref[...] = gj_ * LANES + lo
    h1 = v.astype(BF16)
    r1 = v - h1.astype(F32)
    h2 = r1.astype(BF16)
    h3 = (r1 - h2.astype(F32)).astype(BF16)
    arow = _dot(ohb, h1) + _dot(ohb, h2) + _dot(ohb, h3)
    lane = lax.broadcasted_iota(I32, (cap, LANES), 1)
    gate_ref[...] = jnp.sum(jnp.where(lane == lo, arow, 0.0), axis=-1, keepdims=True)


def _topk(aff_t, cap):
    batch, ne, seq_len = aff_t.shape
    ng = seq_len // LANES
    a4 = aff_t.reshape(batch, ne, ng, LANES)
    grp = pl.BlockSpec((None, None, ng, LANES), lambda b, e: (b, e, 0, 0))
    col = pl.BlockSpec((None, None, cap, 1), lambda b, e: (b, e, 0, 0))
    return pl.pallas_call(
        functools.partial(_topk_kernel, cap=cap),
        grid=(batch, ne),
        in_specs=[grp],
        out_specs=[col, col, grp, grp],
        out_shape=[jax.ShapeDtypeStruct((batch, ne, cap, 1), I32),
                   jax.ShapeDtypeStruct((batch, ne, cap, 1), F32),
                   jax.ShapeDtypeStruct((batch, ne, ng, LANES), I32),
                   jax.ShapeDtypeStruct((batch, ne, ng, LANES), I32)],
        compiler_params=_cparams(("arbitrary", "arbitrary")),
        name="expert_topk",
    )(a4)


FFN_ROWS = 512


def _ffn_kernel(idx_ref, idxn_ref, x_hbm, gate_ref, w1_ref, w3_ref, w2_ref, y_hbm,
                xg_ref, gbuf_ref, acc_ref, sem, osem, *, seq_len):
    b = pl.program_id(0)
    e = pl.program_id(1)
    f = pl.program_id(2)
    ne = pl.num_programs(1)
    blk_id = b * ne + e
    n_blocks = pl.num_programs(0) * ne
    cap = xg_ref.shape[0]
    rb = min(FFN_ROWS, cap)

    def row_copy(iref, base, r):
        return pltpu.make_async_copy(x_hbm.at[pl.ds(base + iref[0, 0, r], 1)], gbuf_ref.at[pl.ds(r, 1)], sem)

    def gather_start(iref, base):
        def body(r, c):
            row_copy(iref, base, r).start()
            return c
        lax.fori_loop(0, cap, body, 0, unroll=8)

    @pl.when(f == 0)
    def _():
        @pl.when(blk_id == 0)
        def _():
            gather_start(idx_ref, b * seq_len)

        def wait_body(r, c):
            row_copy(idx_ref, b * seq_len, r).wait()
            return c
        lax.fori_loop(0, cap, wait_body, 0, unroll=8)
        for blk in range(cap // rb):
            rs = slice(blk * rb, (blk + 1) * rb)
            xg_ref[rs, :] = gbuf_ref[rs, :].astype(BF16)

        @pl.when(blk_id + 1 < n_blocks)
        def _():
            gather_start(idxn_ref, ((blk_id + 1) // ne) * seq_len)

    w1 = w1_ref[...].astype(BF16)
    w3 = w3_ref[...].astype(BF16)
    w2 = w2_ref[...].astype(BF16)
    for blk in range(cap // rb):
        rs = slice(blk * rb, (blk + 1) * rb)
        xg = xg_ref[rs, :]
        h = (jax.nn.silu(_dot(xg, w1)) * _dot(xg, w3)).astype(BF16)
        part = _dot(h, w2)

        @pl.when(f == 0)
        def _():
            acc_ref[rs, :] = part

        @pl.when(f != 0)
        def _():
            acc_ref[rs, :] += part

    @pl.when(f == pl.num_programs(2) - 1)
    def _():
        for blk in range(cap // rb):
            rs = slice(blk * rb, (blk + 1) * rb)
            xg_ref[rs, :] = (acc_ref[rs, :] * gate_ref[rs, :]).astype(BF16)
        out_copy = pltpu.make_async_copy(xg_ref, y_hbm.at[b, e], osem)
        out_copy.start()
        out_copy.wait()


def _ffn(x1, idx, gate, w1, w3, w2, layer, batch, seq_len):
    n, d = x1.shape
    ne, cap = idx.shape[1], idx.shape[2]
    dff = w1.shape[-1]
    fc = 896 if dff % 896 == 0 else dff
    nf = dff // fc
    nblk = batch * ne
    idx3 = idx.reshape(nblk, 1, cap)
    return pl.pallas_call(
        functools.partial(_ffn_kernel, seq_len=seq_len),
        grid=(batch, ne, nf),
        in_specs=[pl.BlockSpec((1, 1, cap), lambda b, e, f: (b * ne + e, 0, 0), memory_space=pltpu.SMEM),
                  pl.BlockSpec((1, 1, cap), lambda b, e, f: (jnp.minimum(b * ne + e + 1, nblk - 1), 0, 0),
                               memory_space=pltpu.SMEM),
                  pl.BlockSpec(memory_space=pl.ANY),
                  pl.BlockSpec((None, None, cap, 1), lambda b, e, f: (b, e, 0, 0)),
                  pl.BlockSpec((None, None, d, fc), lambda b, e, f: (layer, e, 0, f)),
                  pl.BlockSpec((None, None, d, fc), lambda b, e, f: (layer, e, 0, f)),
                  pl.BlockSpec((None, None, fc, d), lambda b, e, f: (layer, e, f, 0))],
        out_specs=pl.BlockSpec(memory_space=pl.ANY),
        out_shape=jax.ShapeDtypeStruct((batch, ne, cap, d), BF16),
        scratch_shapes=[pltpu.VMEM((cap, d), BF16),
                        pltpu.VMEM((cap, d), F32),
                        pltpu.VMEM((cap, d), F32),
                        pltpu.SemaphoreType.DMA(()),
                        pltpu.SemaphoreType.DMA(())],
        compiler_params=_cparams(("arbitrary",) * 3),
        name="expert_ffn",
    )(idx3, idx3, x1, gate, w1, w3, w2)


CMB_WIN = 64
CMB_ALIGN = 16


def _combine_kernel(offs_ref, x_ref, slot_ref, y_hbm, g_ref, b_ref, o_ref, buf_ref, buf2_ref, acc_ref,
                    sem, sem2, *, alpha, cap, tiles_per_seq):
    i = pl.program_id(0)
    nsteps = pl.num_programs(0)
    tc = x_ref.shape[0]
    ne = N_EXPERTS
    kw = ne * CMB_WIN

    def tile(step):
        bidx = step // tiles_per_seq
        return bidx, (bidx * (tiles_per_seq + 1) + step % tiles_per_seq) * ne

    def window(base, e, k):
        start = (offs_ref[base + e] // CMB_ALIGN) * CMB_ALIGN + k * CMB_WIN
        return pl.multiple_of(jnp.minimum(start, cap - CMB_WIN), CMB_ALIGN)

    def first_copy(step, buf, e):
        bidx, base = tile(step)
        return pltpu.make_async_copy(y_hbm.at[bidx, e, pl.ds(window(base, e, 0), CMB_WIN)],
                                     buf_ref.at[buf, pl.ds(e * CMB_WIN, CMB_WIN)], sem.at[buf, e])

    cur = i % 2

    @pl.when(i == 0)
    def _():
        for e in range(ne):
            first_copy(i, cur, e).start()

    @pl.when(i + 1 < nsteps)
    def _():
        for e in range(ne):
            first_copy(i + 1, 1 - cur, e).start()

    bidx, base = tile(i)
    slots = slot_ref[...]
    sp = slots + 1
    el = lax.broadcasted_iota(I32, (ne, kw), 1) // CMB_WIN
    expand = (el == lax.broadcasted_iota(I32, (ne, kw), 0)).astype(BF16)
    sp_exp = (_dot((sp >> 6).astype(F32).astype(BF16), expand) * 64.0
              + _dot((sp & 63).astype(F32).astype(BF16), expand))
    lane1 = lax.broadcasted_iota(I32, (1, kw), 1)
    win_row = jnp.zeros((1, kw), I32)
    for e in range(ne):
        win_row = jnp.where(lane1 // CMB_WIN == e, window(base, e, 0) + 1, win_row)
    oh = (sp_exp - win_row.astype(F32)) == (lane1 % CMB_WIN).astype(F32)
    for e in range(ne):
        first_copy(i, cur, e).wait()
    acc_ref[...] = alpha * x_ref[...] + _dot(oh.astype(BF16), buf_ref[cur])

    lane = lax.broadcasted_iota(I32, (tc, CMB_WIN), 1)
    for e in range(ne):
        off = offs_ref[base + e]
        end = offs_ref[base + ne + e]
        slot = slots[:, e:e + 1]
        n_win = (end - (off // CMB_ALIGN) * CMB_ALIGN + CMB_WIN - 1) // CMB_WIN

        def extra(k, c):
            w0 = window(base, e, k)
            cp = pltpu.make_async_copy(y_hbm.at[bidx, e, pl.ds(w0, CMB_WIN)], buf2_ref, sem2)
            cp.start()
            cp.wait()
            ohk = ((slot - w0) == lane) & ((slot - window(base, e, 0)) >= k * CMB_WIN)
            acc_ref[...] += _dot(ohk.astype(BF16), buf2_ref[...])
            return c

        lax.fori_loop(1, jnp.maximum(n_win, 1), extra, 0)
    o_ref[...] = _layer_norm(acc_ref[...], g_ref[...], b_ref[...])


def _combine(x1, y, slot_tm, offs, g, b, batch, seq_len, alpha):
    n, d = x1.shape
    ne, cap = y.shape[1], y.shape[2]
    tc = min(256, seq_len)
    tps = seq_len // tc
    return pl.pallas_call(
        functools.partial(_combine_kernel, alpha=alpha, cap=cap, tiles_per_seq=tps),
        grid_spec=pltpu.PrefetchScalarGridSpec(
            num_scalar_prefetch=1,
            grid=(n // tc,),
            in_specs=[pl.BlockSpec((tc, d), lambda i, o: (i, 0)),
                      pl.BlockSpec((tc, ne), lambda i, o: (i, 0)),
                      pl.BlockSpec(memory_space=pl.ANY),
                      pl.BlockSpec((1, d), lambda i, o: (0, 0)),
                      pl.BlockSpec((1, d), lambda i, o: (0, 0))],
            out_specs=pl.BlockSpec((tc, d), lambda i, o: (i, 0)),
            scratch_shapes=[pltpu.VMEM((2, ne * CMB_WIN, d), BF16),
                            pltpu.VMEM((CMB_WIN, d), BF16),
                            pltpu.VMEM((tc, d), F32),
                            pltpu.SemaphoreType.DMA((2, ne)),
                            pltpu.SemaphoreType.DMA(())]),
        out_shape=jax.ShapeDtypeStruct((n, d), F32),
        compiler_params=_cparams(("arbitrary",)),
        name="combine_ln",
    )(offs, x1, slot_tm, y, g, b)


def _moe(x1, aff_t, w1, w3, w2, g2, b2, layer, batch, seq_len, alpha):
    cap = EC_FACTOR * seq_len // N_EXPERTS
    idx, gate, slot, offs = _topk(aff_t, cap)
    y = _ffn(x1, idx[..., 0], gate, w1, w3, w2, layer, batch, seq_len)
    tc = min(256, seq_len)
    slot_tm = jnp.transpose(slot.reshape(batch, N_EXPERTS, seq_len), (0, 2, 1)).reshape(batch * seq_len, N_EXPERTS)
    tile_off = offs[:, :, ::tc // LANES, 0]
    tile_off = jnp.concatenate([tile_off, jnp.full((batch, N_EXPERTS, 1), cap, I32)], axis=2)
    tile_off = jnp.transpose(tile_off, (0, 2, 1)).reshape(-1)
    return _combine(x1, y, slot_tm, tile_off, g2, b2, batch, seq_len, alpha)


def kernel(x, positions, w_in, w_out, fno_w, fno_b, lru_conv_w, lru_conv_b, lru_wa, lru_ba, lru_wx, lru_bx,
           lru_lam, dn_conv_w, dn_conv_b, dn_a_log, dn_dt_bias, dn_norm_w, ln1_g, ln1_b, router_w,
           exp_w1, exp_w3, exp_w2, ln2_g, ln2_b):
    batch, seq_len, d = x.shape
    depth = w_in.shape[0]
    gw = GROUP_W
    nh2 = 2 * N_HEADS_G
    alpha = (2.0 * depth) ** 0.25
    n = batch * seq_len

    cos_t, sin_t = _rope_tables(positions)
    tables = _dft_tables(seq_len)

    col = lambda a, b: w_in[:, :, a * gw:b * gw]
    bd = jnp.pad(w_in[:, :, 10 * gw:], ((0, 0), (0, 0), (0, LANES - 2 * nh2)))
    wm = jnp.concatenate([col(0, 1), col(2, 3), col(3, 6), col(9, 10), bd], axis=2).astype(BF16)
    wc = jnp.concatenate([col(1, 2), col(6, 9)], axis=2).astype(BF16)
    cw = jnp.concatenate([lru_conv_w, dn_conv_w], axis=2)
    cb = jnp.concatenate([lru_conv_b, dn_conv_b], axis=1)[:, None, :]
    pad16 = lambda v: jnp.pad(v.reshape(depth, 1, nh2), ((0, 0), (0, 0), (nh2, LANES - 2 * nh2)))
    nalog = pad16(-jnp.exp(dn_a_log.astype(F32)))
    dtb = pad16(dn_dt_bias)
    wf_bd = jax.vmap(_block_diag)(fno_w).astype(BF16)
    bf = fno_b.reshape(depth, 1, gw)
    lru_w = jnp.concatenate([jax.vmap(jax.vmap(_block_diag))(lru_wa),
                             jax.vmap(jax.vmap(_block_diag))(lru_wx)], axis=3).astype(BF16)
    lru_b = jnp.concatenate([lru_ba, lru_bx], axis=2)[:, :, None, :]
    lam = lru_lam[:, :, None, :]
    wo = w_out.astype(BF16)
    nw = jnp.tile(dn_norm_w, (1, N_HEADS_G))[:, None, :]
    rwt = jnp.transpose(router_w, (0, 2, 1)).astype(BF16)

    x2d = x.reshape(n, d)
    for l in range(depth):
        outs = _inproj(x2d, cos_t, sin_t, wm[l], wc[l], cw[l], cb[l], nalog[l], dtb[l], seq_len)
        fa, xc, gg = outs[:3]
        n_att = 3 * len(DIL_PAIRS)
        att_in = outs[3:3 + n_att]
        dq, dk, dv, dsg, dbg = outs[3 + n_att:]
        fy = _fourier(fa, tables, wf_bd[l], bf[l], batch, seq_len)
        yb = _lru(xc, gg, lru_w[l], lru_b[l], lam[l], batch, seq_len)
        att = []
        for p, (window, dil) in enumerate(DIL_PAIRS):
            assert (window // 2) // dil == ATT_R
            att += list(_attn_pattern(*att_in[3 * p:3 * p + 3], dil, batch, seq_len))
        dof, dob = _deltanet(dq, dk, dv, dbg, batch, seq_len)
        x1, aff_t = _outproj(x2d, [fy, yb] + att + [dof, dob, dsg], wo[l], nw[l],
                             ln1_g[l][None, :], ln1_b[l][None, :], rwt[l], batch, seq_len, alpha)
        x2d = _moe(x1, aff_t, exp_w1, exp_w3, exp_w2, ln2_g[l][None, :], ln2_b[l][None, :],
                   l, batch, seq_len, alpha)
    return x2d.reshape(batch, seq_len, d)
```

```python
import functools
import math

import jax
import jax.numpy as jnp
from jax import lax
from jax.experimental import pallas as pl
from jax.experimental.pallas import tpu as pltpu

F32 = jnp.float32
BF16 = jnp.bfloat16
I32 = jnp.int32

GROUP_W = 256
HEAD_DIM = 64
N_HEADS_G = 4
LRU_C = 8.0
DIL_PAIRS = ((128, 1), (512, 4), (2048, 16))
ROPE_THETA = 10000.0
DN_CHUNK = 64
N_EXPERTS = 16
EC_FACTOR = 2
LN_EPS = 1e-5
RMS_EPS = 1e-6
NEG = -1e30

LANES = 128
SUBLANES = 8
VMEM_LIMIT = 56 * 1024 * 1024


def _cparams(sem):
    return pltpu.CompilerParams(dimension_semantics=sem, vmem_limit_bytes=VMEM_LIMIT)


def _dot(a, b):
    return jnp.dot(a, b, preferred_element_type=F32)


def _dot_nt(a, b):
    return lax.dot_general(a, b, (((1,), (1,)), ((), ())), preferred_element_type=F32)


def _bmm(a, b):
    return jnp.einsum('bij,bjk->bik', a, b, preferred_element_type=F32)


def _bmm_nt(a, b):
    return jnp.einsum('bij,bkj->bik', a, b, preferred_element_type=F32)


def _head_sum(t):
    lane = lax.broadcasted_iota(I32, t.shape, 1)
    head = lane // HEAD_DIM
    out = jnp.zeros_like(t)
    for h in range(N_HEADS_G):
        m = head == h
        s = jnp.sum(jnp.where(m, t, 0.0), axis=-1, keepdims=True)
        out = jnp.where(m, s, out)
    return out


def _layer_norm(v, g, b):
    mu = jnp.mean(v, axis=-1, keepdims=True)
    c = v - mu
    var = jnp.mean(c * c, axis=-1, keepdims=True)
    return c * lax.rsqrt(var + LN_EPS) * g + b


def _rope_table_kernel(pos_ref, inv_ref, sgn_ref, cos_ref, sin_ref):
    ang = pos_ref[...] * inv_ref[...]
    cos_ref[...] = jnp.cos(ang)
    sin_ref[...] = jnp.sin(ang) * sgn_ref[...]


def _rope_tables(positions):
    n = positions.size
    half = HEAD_DIM // 2
    inv = ROPE_THETA ** (-jnp.arange(half, dtype=F32) / half)
    inv = jnp.tile(inv, LANES // half)[None, :]
    lane = jnp.arange(LANES)
    sgn = jnp.where((lane % HEAD_DIM) < half, -1.0, 1.0).astype(F32)[None, :]
    pos = positions.reshape(n, 1).astype(F32)
    tm = min(2048, n)
    return pl.pallas_call(
        _rope_table_kernel,
        grid=(n // tm,),
        in_specs=[pl.BlockSpec((tm, 1), lambda i: (i, 0)),
                  pl.BlockSpec((1, LANES), lambda i: (0, 0)),
                  pl.BlockSpec((1, LANES), lambda i: (0, 0))],
        out_specs=[pl.BlockSpec((tm, LANES), lambda i: (i, 0))] * 2,
        out_shape=[jax.ShapeDtypeStruct((n, LANES), F32)] * 2,
        compiler_params=_cparams(("arbitrary",)),
        name="rope_tables",
    )(pos, inv, sgn)


def _inproj_kernel(x_ref, xp_ref, xn_ref, wm_ref, wc_ref, cw_ref, cb_ref, cos_ref, sin_ref,
                   nalog_ref, dtb_ref,
                   fa_ref, xc_ref, gg_ref, *rest, tiles_per_seq):
    n_att = 3 * len(DIL_PAIRS)
    att_refs = rest[:n_att]
    dq_ref, dk_ref, dv_ref, dsg_ref, dbg_ref, scr_ref = rest[n_att:]
    i = pl.program_id(0)
    tm = x_ref.shape[0]
    gw = GROUP_W
    first = (i % tiles_per_seq) == 0
    last = (i % tiles_per_seq) == tiles_per_seq - 1
    xb = x_ref[...].astype(BF16)
    hm = _dot(xb, wm_ref[...])
    hc = _dot(xb, wc_ref[...])
    hp = _dot(xp_ref[...].astype(BF16), wc_ref[...]) * jnp.where(first, 0.0, 1.0)
    hn = _dot(xn_ref[...].astype(BF16), wc_ref[...]) * jnp.where(last, 0.0, 1.0)
    ext = jnp.concatenate([hp, hc, hn], axis=0)
    conv = cb_ref[...]
    for j in range(4):
        conv = conv + cw_ref[j:j + 1, :] * ext[6 + j:6 + j + tm, :]

    fa_ref[...] = hm[:, 0:gw]
    xc_ref[...] = conv[:, 0:gw]
    gg_ref[...] = jax.nn.gelu(hm[:, gw:2 * gw])

    cos2 = jnp.concatenate([cos_ref[...], cos_ref[...]], axis=1)
    sin2 = jnp.concatenate([sin_ref[...], sin_ref[...]], axis=1)
    lane = lax.broadcasted_iota(I32, (tm, gw), 1)
    lo = (lane % HEAD_DIM) < (HEAD_DIM // 2)

    def rope(t):
        rot = jnp.where(lo, pltpu.roll(t, gw - HEAD_DIM // 2, 1), pltpu.roll(t, HEAD_DIM // 2, 1))
        return t * cos2 + rot * sin2

    qkv_att = (rope(hm[:, 2 * gw:3 * gw]) * (HEAD_DIM ** -0.5), rope(hm[:, 3 * gw:4 * gw]), hm[:, 4 * gw:5 * gw])
    nl = gw // LANES
    for j, val in enumerate(qkv_att):
        for c in range(nl):
            scr_ref[j, c] = val[:, c * LANES:(c + 1) * LANES]
    for p, (_, dil) in enumerate(DIL_PAIRS):
        for j in range(3):
            ref = att_refs[3 * p + j]
            if dil == 1:
                ref[...] = qkv_att[j].astype(BF16)
            else:
                for r in range(dil):
                    for c in range(nl):
                        c0 = r * gw + c * LANES
                        ref[:, c0:c0 + LANES] = scr_ref[j, c, pl.ds(r, tm // dil, stride=dil), :].astype(BF16)

    qkv = jax.nn.silu(conv[:, gw:])

    def l2n(t):
        return t * lax.rsqrt(_head_sum(t * t) + RMS_EPS)

    dq_ref[...] = l2n(qkv[:, 0:gw])
    dk_ref[...] = l2n(qkv[:, gw:2 * gw])
    dv_ref[...] = qkv[:, 2 * gw:3 * gw]
    dsg_ref[...] = jax.nn.silu(hm[:, 5 * gw:6 * gw])

    bd = hm[:, 6 * gw:6 * gw + LANES]
    l128 = lax.broadcasted_iota(I32, (tm, LANES), 1)
    beta = jax.nn.sigmoid(bd)
    g = nalog_ref[...] * jax.nn.softplus(bd + dtb_ref[...])
    nh2 = 2 * N_HEADS_G
    dbg_ref[...] = jnp.where(l128 < nh2, beta, jnp.where(l128 < 2 * nh2, g, 0.0))


def _inproj(x2d, cos_t, sin_t, wm, wc, cw, cb, nalog, dtb, seq_len):
    n, d = x2d.shape
    tm = min(512, seq_len)
    nt = n // tm
    tps = seq_len // tm
    hb = tm // SUBLANES
    gw = GROUP_W
    full = lambda shape: pl.BlockSpec(shape, lambda i: (0,) * len(shape))
    row = lambda w: pl.BlockSpec((tm, w), lambda i: (i, 0))
    att_specs, att_shapes = [], []
    for _, dil in DIL_PAIRS:
        att_specs += [pl.BlockSpec((tm // dil, dil * gw), lambda i: (i, 0))] * 3
        att_shapes += [jax.ShapeDtypeStruct((n // dil, dil * gw), BF16)] * 3
    f32_out = lambda w: jax.ShapeDtypeStruct((n, w), F32)
    out_specs = [row(gw)] * 3 + att_specs + [row(gw)] * 4 + [row(LANES)]
    out_shape = [f32_out(gw)] * 3 + att_shapes + [f32_out(gw)] * 4 + [f32_out(LANES)]
    return pl.pallas_call(
        functools.partial(_inproj_kernel, tiles_per_seq=tps),
        grid=(nt,),
        in_specs=[row(d),
                  pl.BlockSpec((SUBLANES, d), lambda i: (jnp.maximum(i * hb - 1, 0), 0)),
                  pl.BlockSpec((SUBLANES, d), lambda i: (jnp.minimum((i + 1) * hb, nt * hb - 1), 0)),
                  full(wm.shape), full(wc.shape), full(cw.shape), full(cb.shape),
                  row(LANES), row(LANES), full(nalog.shape), full(dtb.shape)],
        out_specs=out_specs,
        out_shape=out_shape,
        scratch_shapes=[pltpu.VMEM((3, gw // LANES, tm, LANES), F32)],
        compiler_params=_cparams(("arbitrary",)),
        name="inproj",
    )(x2d, x2d, x2d, wm, wc, cw, cb, cos_t, sin_t, nalog, dtb)


def _fourier1_kernel(x_ref, t_ref, o_ref):
    for j in range(SUBLANES):
        xj = x_ref[:, j, :].astype(BF16)
        o_ref[j] = _dot(t_ref[j], xj)


def _fourier2_kernel(re_ref, im_ref, w2_ref, cc_ref, sc_ref, wf_ref, bf_ref, o_ref, *, scale):
    n2 = re_ref.shape[0]
    for j in range(SUBLANES):
        a = jnp.concatenate([re_ref[:, j, :], im_ref[:, j, :]], axis=0).astype(BF16)
        z = _dot(w2_ref[...], a)
        zr = z[0:n2].astype(BF16)
        zi = z[n2:2 * n2].astype(BF16)
        f = (_dot(zr, cc_ref[...]) + _dot(zi, sc_ref[...])) * scale
        o_ref[:, j, :] = _dot(f.astype(BF16), wf_ref[...]) + bf_ref[...]


def _dft_tables(seq_len):
    n1 = 1 << (int(math.log2(seq_len)) // 2)
    n2 = seq_len // n1
    k1 = jnp.arange(n1, dtype=I32)
    s1 = jnp.arange(n1, dtype=I32)
    s2 = jnp.arange(n2, dtype=I32)
    m = (k1[None, :, None] * (s1[None, None, :] * n2 + s2[:, None, None])) % seq_len
    ang = m.astype(F32) * (2.0 * math.pi / seq_len)
    t1 = jnp.concatenate([jnp.cos(ang), -jnp.sin(ang)], axis=1).astype(BF16)
    k2 = jnp.arange(n2, dtype=I32)
    m2 = (k2[:, None] * s2[None, :]) % n2
    a2 = m2.astype(F32) * (2.0 * math.pi / n2)
    c2, sn2 = jnp.cos(a2), jnp.sin(a2)
    w2 = jnp.concatenate([jnp.concatenate([c2, sn2], axis=1),
                          jnp.concatenate([-sn2, c2], axis=1)], axis=0).astype(BF16)
    c = jnp.arange(HEAD_DIM, dtype=I32)
    ac = ((c[:, None] * c[None, :]) % HEAD_DIM).astype(F32) * (2.0 * math.pi / HEAD_DIM)
    eye = jnp.eye(N_HEADS_G, dtype=F32)
    cc = jnp.kron(eye, jnp.cos(ac)).astype(BF16)
    sc = jnp.kron(eye, jnp.sin(ac)).astype(BF16)
    return n1, n2, t1, w2, cc, sc


def _block_diag(w):
    h, a, b = w.shape
    eye = jnp.eye(h, dtype=w.dtype)
    return (eye[:, None, :, None] * w[:, :, None, :]).reshape(h * a, h * b)


def _fourier(fa, tables, wf_bd, bf, batch, seq_len):
    n1, n2, t1, w2, cc, sc = tables
    gw = GROUP_W
    x4 = fa.reshape(batch, n1, n2, gw)
    a = pl.pallas_call(
        _fourier1_kernel,
        grid=(batch, n2 // SUBLANES),
        in_specs=[pl.BlockSpec((None, n1, SUBLANES, gw), lambda b, i: (b, 0, i, 0)),
                  pl.BlockSpec((SUBLANES, 2 * n1, n1), lambda b, i: (i, 0, 0))],
        out_specs=pl.BlockSpec((None, SUBLANES, 2 * n1, gw), lambda b, i: (b, i, 0, 0)),
        out_shape=jax.ShapeDtypeStruct((batch, n2, 2 * n1, gw), F32),
        compiler_params=_cparams(("arbitrary", "arbitrary")),
        name="fourier_stage1",
    )(x4, t1)
    nb = n1 // SUBLANES
    full = lambda arr: pl.BlockSpec(arr.shape, lambda b, i: (0,) * arr.ndim)
    y = pl.pallas_call(
        functools.partial(_fourier2_kernel, scale=1.0 / math.sqrt(seq_len * HEAD_DIM)),
        grid=(batch, nb),
        in_specs=[pl.BlockSpec((None, n2, SUBLANES, gw), lambda b, i: (b, 0, i, 0)),
                  pl.BlockSpec((None, n2, SUBLANES, gw), lambda b, i: (b, 0, nb + i, 0)),
                  full(w2), full(cc), full(sc), full(wf_bd), full(bf)],
        out_specs=pl.BlockSpec((None, n2, SUBLANES, gw), lambda b, i: (b, 0, i, 0)),
        out_shape=jax.ShapeDtypeStruct((batch, n2, n1, gw), F32),
        compiler_params=_cparams(("arbitrary", "arbitrary")),
        name="fourier_stage2",
    )(a, a, w2, cc, sc, wf_bd, bf)
    return y.reshape(batch * seq_len, gw)


def _lru_kernel(*refs, rev):
    if rev:
        xc_ref, w_ref, b_ref, lam_ref, hf_ref, gg_ref, o_ref, carry_ref = refs
    else:
        xc_ref, w_ref, b_ref, lam_ref, o_ref, carry_ref = refs
    gw = GROUP_W
    t = xc_ref.shape[0]

    @pl.when(pl.program_id(1) == 0)
    def _():
        carry_ref[...] = jnp.zeros_like(carry_ref)

    xc = xc_ref[...]
    gates = _dot(xc.astype(BF16), w_ref[...]) + b_ref[...]
    r = jax.nn.sigmoid(gates[:, 0:gw])
    ig = jax.nn.sigmoid(gates[:, gw:2 * gw])
    log_a = -LRU_C * r * jax.nn.softplus(-lam_ref[...])
    a = jnp.exp(log_a)
    th = jnp.tanh(log_a)
    u = jnp.sqrt(-2.0 * th / (1.0 - th)) * (ig * xc)

    r8 = lax.broadcasted_iota(I32, (t, gw), 0) % SUBLANES
    for s in (1, 2, 4):
        if rev:
            a_s, u_s, m = pltpu.roll(a, t - s, 0), pltpu.roll(u, t - s, 0), r8 < SUBLANES - s
        else:
            a_s, u_s, m = pltpu.roll(a, s, 0), pltpu.roll(u, s, 0), r8 >= s
        u = jnp.where(m, a * u_s + u, u)
        a = jnp.where(m, a * a_s, a)
    carry = carry_ref[...]
    ng = t // SUBLANES
    blocks = [None] * ng
    for g in (range(ng - 1, -1, -1) if rev else range(ng)):
        sl = slice(g * SUBLANES, (g + 1) * SUBLANES)
        blk = u[sl] + a[sl] * carry
        carry = blk[0:1] if rev else blk[SUBLANES - 1:SUBLANES]
        blocks[g] = blk
    carry_ref[...] = carry
    h = jnp.concatenate(blocks, axis=0)
    if rev:
        o_ref[...] = (hf_ref[...] + h) * gg_ref[...]
    else:
        o_ref[...] = h


def _lru(xc, gg, w_gate, b_gate, lam, batch, seq_len):
    n, gw = xc.shape
    t = min(512, seq_len)
    nt = seq_len // t
    fwd = lambda b, i: (b * nt + i, 0)
    bwd = lambda b, i: (b * nt + (nt - 1 - i), 0)
    outs = None
    for d, imap in ((0, fwd), (1, bwd)):
        par = lambda arr: pl.BlockSpec((None,) + arr.shape[1:], lambda b, i, d=d: (d,) + (0,) * (arr.ndim - 1))
        tile = pl.BlockSpec((t, gw), imap)
        ins = [xc, w_gate, b_gate, lam]
        specs = [tile, par(w_gate), par(b_gate), par(lam)]
        if d == 1:
            ins += [outs, gg]
            specs += [tile, tile]
        outs = pl.pallas_call(
            functools.partial(_lru_kernel, rev=bool(d)),
            grid=(batch, nt),
            in_specs=specs,
            out_specs=tile,
            out_shape=jax.ShapeDtypeStruct((n, gw), F32),
            scratch_shapes=[pltpu.VMEM((1, gw), F32)],
            compiler_params=_cparams(("arbitrary", "arbitrary")),
            name="lru_bwd" if d else "lru_fwd",
        )(*ins)
    return outs


ATT_R = 64
ATT_QB = 128


def _attn_kernel(q_ref, kp_ref, k_ref, kn_ref, vp_ref, v_ref, vn_ref, o_ref, l_ref, *, sub_len):
    tq = q_ref.shape[0]
    base = pl.program_id(2) * tq
    kext = jnp.concatenate([kp_ref[...], k_ref[...], kn_ref[...]], axis=0)
    vext = jnp.concatenate([vp_ref[...], v_ref[...], vn_ref[...]], axis=0)
    qb = min(ATT_QB, tq)
    kw = qb + 2 * ATT_R
    nh = N_HEADS_G
    nq = tq // qb
    order = [(jq, h) for jq in range(nq) for h in range(nh)]
    hs = lambda h: slice(h * HEAD_DIM, (h + 1) * HEAD_DIM)
    q3 = jnp.stack([q_ref[jq * qb:(jq + 1) * qb, hs(h)] for jq, h in order])
    k3 = jnp.stack([kext[jq * qb:jq * qb + kw, hs(h)] for jq, h in order])
    v3 = jnp.stack([vext[jq * qb:jq * qb + kw, hs(h)] for jq, h in order])
    shape = (nq * nh, qb, kw)
    qi = lax.broadcasted_iota(I32, shape, 1)
    kj = lax.broadcasted_iota(I32, shape, 2)
    jqi = lax.broadcasted_iota(I32, shape, 0) // nh
    kpos = base + jqi * qb + kj - ATT_R
    mask = (jnp.abs(kj - ATT_R - qi) <= ATT_R) & (kpos >= 0) & (kpos < sub_len)
    s = jnp.where(mask, _bmm_nt(q3, k3), NEG)
    m = jnp.max(s, axis=-1, keepdims=True)
    p = jnp.exp(s - m)
    l = jnp.sum(p, axis=-1, keepdims=True)
    o3 = _bmm(p.astype(BF16), v3) / l
    lse3 = jnp.broadcast_to(m + jnp.log(l), (nq * nh, qb, HEAD_DIM))
    for jq in range(nq):
        o_ref[jq * qb:(jq + 1) * qb, :] = jnp.concatenate([o3[jq * nh + h] for h in range(nh)], axis=1)
        l_ref[jq * qb:(jq + 1) * qb, :] = jnp.concatenate([lse3[jq * nh + h] for h in range(nh)], axis=1)


def _attn_pattern(aq, ak, av, dil, batch, seq_len):
    gw = GROUP_W
    sub = seq_len // dil
    q3, k3, v3 = (t.reshape(batch, sub, dil * gw) for t in (aq, ak, av))
    tq = min(512, sub)
    nq = sub // tq
    hb = tq // ATT_R
    nhb = sub // ATT_R
    main = pl.BlockSpec((None, tq, gw), lambda b, r, i: (b, i, r))
    prev = pl.BlockSpec((None, ATT_R, gw), lambda b, r, i: (b, jnp.maximum(i * hb - 1, 0), r))
    nxt = pl.BlockSpec((None, ATT_R, gw), lambda b, r, i: (b, jnp.minimum((i + 1) * hb, nhb - 1), r))
    o, l = pl.pallas_call(
        functools.partial(_attn_kernel, sub_len=sub),
        grid=(batch, dil, nq),
        in_specs=[main, prev, main, nxt, prev, main, nxt],
        out_specs=[main, main],
        out_shape=[jax.ShapeDtypeStruct((batch, sub, dil * gw), F32)] * 2,
        compiler_params=_cparams(("arbitrary", "arbitrary", "arbitrary")),
        name=f"dilated_attn_d{dil}",
    )(q3, k3, k3, k3, v3, v3, v3)
    return o.reshape(batch * sub, dil * gw), l.reshape(batch * sub, dil * gw)


DN_TILE = 256


def _deltanet_kernel(qf_ref, kf_ref, vf_ref, gf_ref, qb_ref, kb_ref, vb_ref, gb_ref,
                     of_ref, ob_ref, state_ref):
    nbat, t = qf_ref.shape[0], qf_ref.shape[1]
    cl = DN_CHUNK
    nc = t // cl
    nh = N_HEADS_G
    hd = HEAD_DIM
    gw = GROUP_W
    nb = nbat * 2 * nc

    @pl.when(pl.program_id(0) == 0)
    def _():
        state_ref[...] = jnp.zeros_like(state_ref)

    rc = lax.broadcasted_iota(I32, (t, LANES), 0) % cl
    gcs, gts = [], []
    for d, g_ref in ((0, gf_ref), (1, gb_ref)):
        gcl, gtl = [], []
        for bi in range(nbat):
            gc = g_ref[bi]
            s = 1
            while s < cl:
                if d:
                    gc = jnp.where(rc < cl - s, gc + pltpu.roll(gc, t - s, 0), gc)
                else:
                    gc = jnp.where(rc >= s, gc + pltpu.roll(gc, s, 0), gc)
                s *= 2
            gcl.append(gc)
            gtl.append(gc.T)
        gcs.append(gcl)
        gts.append(gtl)

    order = [(bi, d, c) for bi in range(nbat) for d in range(2) for c in range(nc)]
    data = ((qf_ref, kf_ref, vf_ref, gf_ref), (qb_ref, kb_ref, vb_ref, gb_ref))
    rows = lambda c: slice(c * cl, (c + 1) * cl)

    q3 = jnp.stack([data[d][0][bi, rows(c), :] for bi, d, c in order])
    k3 = jnp.stack([data[d][1][bi, rows(c), :] for bi, d, c in order])
    v3 = jnp.stack([data[d][2][bi, rows(c), :] for bi, d, c in order])

    def lane_expand(src_of, col_of):
        blocks = []
        for bi, d, c in order:
            src = src_of(d, bi)[rows(c), :]
            blocks.append(jnp.concatenate(
                [jnp.broadcast_to(src[:, col_of(d, h):col_of(d, h) + 1], (cl, hd)) for h in range(nh)], axis=1))
        return jnp.stack(blocks)

    g_col = lambda d, h: 2 * nh + d * nh + h
    beta3 = lane_expand(lambda d, bi: data[d][3][bi], lambda d, h: d * nh + h)
    gcc3 = lane_expand(lambda d, bi: gcs[d][bi], g_col)
    gcr3 = jnp.stack([jnp.concatenate([gts[d][bi][g_col(d, h):g_col(d, h) + 1, rows(c)] for h in range(nh)], axis=1)
                      for bi, d, c in order])

    ri = lax.broadcasted_iota(I32, (nb, cl, gw), 1)
    ci = lax.broadcasted_iota(I32, (nb, cl, gw), 2) % hd
    isb = (lax.broadcasted_iota(I32, (nb, cl, gw), 0) // nc) % 2 == 1
    r2 = jnp.where(isb, ci, ri)
    c2 = jnp.where(isb, ri, ci)
    incl = r2 >= c2
    strict = r2 > c2
    eye = (ri == ci).astype(F32)

    def block_diag(x):
        n = x.shape[0]
        same_head = (lax.broadcasted_iota(I32, (n, gw, gw), 1) // hd) == (lax.broadcasted_iota(I32, (n, gw, gw), 2) // hd)
        return jnp.where(same_head, jnp.concatenate([x.astype(BF16)] * nh, axis=1), jnp.zeros((), BF16))

    decay = jnp.where(incl, jnp.exp(jnp.where(incl, gcc3 - gcr3, 0.0)), 0.0)
    kb3 = k3 * beta3
    vb3 = v3 * beta3
    kbd = block_diag(k3)
    m = _bmm_nt(kb3.astype(BF16), kbd) * jnp.where(strict, decay, 0.0)
    xp = -m
    tm = eye + xp
    for _ in range(5):
        xp = _bmm(xp.astype(BF16), block_diag(xp))
        tm = tm + _bmm(tm.astype(BF16), block_diag(xp))
    eg = jnp.exp(gcc3)
    tmb = tm.astype(BF16)
    w3 = _bmm(tmb, block_diag(vb3))
    u3 = _bmm(tmb, block_diag(kb3 * eg)).astype(BF16)
    qs = q3 * (hd ** -0.5)
    a3 = (_bmm_nt(qs.astype(BF16), kbd) * decay).astype(BF16)
    qd3 = (qs * eg).astype(BF16)
    isb1 = (lax.broadcasted_iota(I32, (nb, 1, gw), 0) // nc) % 2 == 1
    gl3 = jnp.where(isb1, gcc3[:, 0:1, :], gcc3[:, cl - 1:cl, :])
    kd3 = k3 * jnp.exp(gl3 - gcc3)
    kdt3 = _bmm_nt(eye.astype(BF16), block_diag(kd3)).astype(BF16)
    egl3 = jnp.exp(gl3)

    def step_bodies(arr, c):
        parts = []
        for bi in range(nbat):
            f0 = bi * 2 * nc + c
            b0 = bi * 2 * nc + nc + (nc - 1 - c)
            parts += [arr[f0:f0 + 1], arr[b0:b0 + 1]]
        return jnp.concatenate(parts, axis=0)

    st = state_ref[...]
    for c in range(nc):
        sbd = block_diag(st)
        v_new = step_bodies(w3, c) - _bmm(step_bodies(u3, c), sbd)
        vbd = block_diag(v_new)
        o2 = _bmm(step_bodies(qd3, c), sbd) + _bmm(step_bodies(a3, c), vbd)
        st = st * step_bodies(egl3, c) + _bmm(step_bodies(kdt3, c), vbd)
        for bi in range(nbat):
            of_ref[bi, rows(c), :] = o2[2 * bi]
            ob_ref[bi, rows(nc - 1 - c), :] = o2[2 * bi + 1]
    state_ref[...] = st


def _deltanet(dq, dk, dv, dbg, batch, seq_len):
    n, gw = dq.shape
    t = min(DN_TILE, seq_len)
    nt = seq_len // t
    fwd = lambda i: (0, i, 0)
    bwd = lambda i: (0, nt - 1 - i, 0)
    tf, tb = pl.BlockSpec((batch, t, gw), fwd), pl.BlockSpec((batch, t, gw), bwd)
    gf, gb = pl.BlockSpec((batch, t, LANES), fwd), pl.BlockSpec((batch, t, LANES), bwd)
    dq, dk, dv = (a.reshape(batch, seq_len, gw) for a in (dq, dk, dv))
    dbg = dbg.reshape(batch, seq_len, LANES)
    outs = pl.pallas_call(
        _deltanet_kernel,
        grid=(nt,),
        in_specs=[tf, tf, tf, gf, tb, tb, tb, gb],
        out_specs=[tf, tb],
        out_shape=[jax.ShapeDtypeStruct((batch, seq_len, gw), F32)] * 2,
        scratch_shapes=[pltpu.VMEM((2 * batch, HEAD_DIM, GROUP_W), F32)],
        compiler_params=_cparams(("arbitrary",)),
        name="deltanet",
    )(dq, dk, dv, dbg, dq, dk, dv, dbg)
    return [o.reshape(n, gw) for o in outs]


def _outproj_kernel(x_ref, fy_ref, yb_ref, o1_ref, l1_ref, o2_ref, l2_ref, o3_ref, l3_ref,
                    dof_ref, dob_ref, dsg_ref, wo_ref, nw_ref, g_ref, b_ref, rwt_ref,
                    x1_ref, aff_ref, scr_ref, *, alpha):
    gw = GROUP_W
    tm = x_ref.shape[0]
    att = []
    for p, ((_, dil), refs) in enumerate(zip(DIL_PAIRS, ((o1_ref, l1_ref), (o2_ref, l2_ref), (o3_ref, l3_ref)))):
        for j, ref in enumerate(refs):
            if dil == 1:
                att.append(ref[...])
            else:
                nl = gw // LANES
                for r in range(dil):
                    for c in range(nl):
                        c0 = r * gw + c * LANES
                        scr_ref[2 * p + j, c, pl.ds(r, tm // dil, stride=dil), :] = ref[:, c0:c0 + LANES]
                att.append(jnp.concatenate([scr_ref[2 * p + j, c] for c in range(nl)], axis=1))
    o1, l1, o2, l2, o3, l3 = att
    m = jnp.maximum(jnp.maximum(l1, l2), l3)
    w1, w2, w3 = jnp.exp(l1 - m), jnp.exp(l2 - m), jnp.exp(l3 - m)
    yc = (w1 * o1 + w2 * o2 + w3 * o3) / (w1 + w2 + w3)
    od = dof_ref[...] + dob_ref[...]
    ms = _head_sum(od * od) * (1.0 / HEAD_DIM)
    yd = od * lax.rsqrt(ms + RMS_EPS) * nw_ref[...] * dsg_ref[...]
    y = (_dot(fy_ref[...].astype(BF16), wo_ref[0:gw, :])
         + _dot(yb_ref[...].astype(BF16), wo_ref[gw:2 * gw, :])
         + _dot(yc.astype(BF16), wo_ref[2 * gw:3 * gw, :])
         + _dot(yd.astype(BF16), wo_ref[3 * gw:4 * gw, :]))
    x1 = _layer_norm(alpha * x_ref[...] + y, g_ref[...], b_ref[...])
    x1_ref[...] = x1
    logits = _dot_nt(rwt_ref[...], x1.astype(BF16))
    mx = jnp.max(logits, axis=0, keepdims=True)
    e = jnp.exp(logits - mx)
    aff_ref[...] = e / jnp.sum(e, axis=0, keepdims=True)


def _outproj(x2d, parts, wo, nw, g, b, rwt, batch, seq_len, alpha):
    n, d = x2d.shape
    gw = GROUP_W
    tm = min(512, seq_len)
    tps = seq_len // tm
    row = lambda w: pl.BlockSpec((tm, w), lambda i: (i, 0))
    full = lambda arr: pl.BlockSpec(arr.shape, lambda i: (0,) * arr.ndim)
    att_specs = []
    for _, dil in DIL_PAIRS:
        att_specs += [pl.BlockSpec((tm // dil, dil * gw), lambda i: (i, 0))] * 2
    return pl.pallas_call(
        functools.partial(_outproj_kernel, alpha=alpha),
        grid=(n // tm,),
        in_specs=[row(d)] + [row(gw)] * 2 + att_specs + [row(gw)] * 3
                 + [full(wo), full(nw), full(g), full(b), full(rwt)],
        out_specs=[row(d), pl.BlockSpec((None, N_EXPERTS, tm), lambda i: (i // tps, 0, i % tps))],
        out_shape=[jax.ShapeDtypeStruct((n, d), F32),
                   jax.ShapeDtypeStruct((batch, N_EXPERTS, seq_len), F32)],
        scratch_shapes=[pltpu.VMEM((2 * len(DIL_PAIRS), gw // LANES, tm, LANES), F32)],
        compiler_params=_cparams(("arbitrary",)),
        name="outproj_ln_router",
    )(x2d, *parts, wo, nw, g, b, rwt)


def _topk_kernel(a_ref, idx_ref, gate_ref, slot_ref, off_ref, *, cap):
    ng = a_ref.shape[0]
    v = a_ref[...]
    bits = pltpu.bitcast(v, I32)

    thr = jnp.int32(0)
    for shift in range(28, -1, -4):
        n_cand = 7 if shift == 28 else 15
        digit = jnp.int32(0)
        for c in range(1, n_cand + 1):
            cnt = jnp.sum((bits >= (thr | jnp.int32(c << shift))).astype(I32))
            digit = digit + (cnt >= cap).astype(I32)
        thr = thr | (digit << shift)
    gt = bits > thr
    eq = bits == thr
    need_eq = cap - jnp.sum(gt.astype(I32))

    li = lax.broadcasted_iota(I32, (LANES, LANES), 0)
    lj = lax.broadcasted_iota(I32, (LANES, LANES), 1)
    ut_incl = (li <= lj).astype(BF16)
    gi = lax.broadcasted_iota(I32, (ng, ng), 0)
    gj = lax.broadcasted_iota(I32, (ng, ng), 1)
    lt_strict = (gj < gi).astype(BF16)

    def prefix(mask):
        p1 = _dot(mask.astype(BF16), ut_incl)
        tot = jnp.broadcast_to(p1[:, LANES - 1:LANES], (ng, LANES))
        return p1, _dot(lt_strict, tot.astype(BF16)), tot

    p1e, offe, _ = prefix(eq)
    rank_eq = p1e - eq.astype(F32) + offe
    sel = gt | (eq & (rank_eq < need_eq.astype(F32)))
    p1, offs, tot = prefix(sel)
    slot_ref[...] = jnp.where(sel, (p1 + offs).astype(I32) - 1, -1)
    off_ref[...] = offs.astype(I32)

    j = lax.broadcasted_iota(I32, (cap, 1), 0).astype(F32)
    ends_row = (offs + tot).T[0:1, :]
    offs_row = offs.T[0:1, :]
    gj_ = jnp.sum((ends_row <= j).astype(I32), axis=-1, keepdims=True)
    oh = lax.broadcasted_iota(I32, (cap, ng), 1) == gj_
    off_j = jnp.sum(jnp.where(oh, offs_row, 0.0), axis=-1, keepdims=True)
    ohb = oh.astype(BF16)
    prow = _dot(ohb, p1.astype(BF16))
    lo = jnp.sum((prow <= (j - off_j)).astype(I32), axis=-1, keepdims=True)
    idx_ref[...] = gj_ * LANES + lo
    h1 = v.astype(BF16)
    r1 = v - h1.astype(F32)
    h2 = r1.astype(BF16)
    h3 = (r1 - h2.astype(F32)).astype(BF16)
    arow = _dot(ohb, h1) + _dot(ohb, h2) + _dot(ohb, h3)
    lane = lax.broadcasted_iota(I32, (cap, LANES), 1)
    gate_ref[...] = jnp.sum(jnp.where(lane == lo, arow, 0.0), axis=-1, keepdims=True)


def _topk(aff_t, cap):
    batch, ne, seq_len = aff_t.shape
    ng = seq_len // LANES
    a4 = aff_t.reshape(batch, ne, ng, LANES)
    grp = pl.BlockSpec((None, None, ng, LANES), lambda b, e: (b, e, 0, 0))
    col = pl.BlockSpec((None, None, cap, 1), lambda b, e: (b, e, 0, 0))
    return pl.pallas_call(
        functools.partial(_topk_kernel, cap=cap),
        grid=(batch, ne),
        in_specs=[grp],
        out_specs=[col, col, grp, grp],
        out_shape=[jax.ShapeDtypeStruct((batch, ne, cap, 1), I32),
                   jax.ShapeDtypeStruct((batch, ne, cap, 1), F32),
                   jax.ShapeDtypeStruct((batch, ne, ng, LANES), I32),
                   jax.ShapeDtypeStruct((batch, ne, ng, LANES), I32)],
        compiler_params=_cparams(("arbitrary", "arbitrary")),
        name="expert_topk",
    )(a4)


FFN_ROWS = 512


def _ffn_kernel(idx_ref, idxn_ref, x_hbm, gate_ref, w1_ref, w3_ref, w2_ref, y_hbm,
                xg_ref, gbuf_ref, acc_ref, sem, osem, *, seq_len):
    b = pl.program_id(0)
    e = pl.program_id(1)
    f = pl.program_id(2)
    ne = pl.num_programs(1)
    blk_id = b * ne + e
    n_blocks = pl.num_programs(0) * ne
    cap = xg_ref.shape[0]
    rb = min(FFN_ROWS, cap)

    def row_copy(iref, base, r):
        return pltpu.make_async_copy(x_hbm.at[pl.ds(base + iref[0, 0, r], 1)], gbuf_ref.at[pl.ds(r, 1)], sem)

    def gather_start(iref, base):
        def body(r, c):
            row_copy(iref, base, r).start()
            return c
        lax.fori_loop(0, cap, body, 0, unroll=8)

    @pl.when(f == 0)
    def _():
        @pl.when(blk_id == 0)
        def _():
            gather_start(idx_ref, b * seq_len)

        def wait_body(r, c):
            row_copy(idx_ref, b * seq_len, r).wait()
            return c
        lax.fori_loop(0, cap, wait_body, 0, unroll=8)
        for blk in range(cap // rb):
            rs = slice(blk * rb, (blk + 1) * rb)
            xg_ref[rs, :] = gbuf_ref[rs, :].astype(BF16)

        @pl.when(blk_id + 1 < n_blocks)
        def _():
            gather_start(idxn_ref, ((blk_id + 1) // ne) * seq_len)

    w1 = w1_ref[...].astype(BF16)
    w3 = w3_ref[...].astype(BF16)
    w2 = w2_ref[...].astype(BF16)
    for blk in range(cap // rb):
        rs = slice(blk * rb, (blk + 1) * rb)
        xg = xg_ref[rs, :]
        h = (jax.nn.silu(_dot(xg, w1)) * _dot(xg, w3)).astype(BF16)
        part = _dot(h, w2)

        @pl.when(f == 0)
        def _():
            acc_ref[rs, :] = part

        @pl.when(f != 0)
        def _():
            acc_ref[rs, :] += part

    @pl.when(f == pl.num_programs(2) - 1)
    def _():
        for blk in range(cap // rb):
            rs = slice(blk * rb, (blk + 1) * rb)
            xg_ref[rs, :] = (acc_ref[rs, :] * gate_ref[rs, :]).astype(BF16)
        out_copy = pltpu.make_async_copy(xg_ref, y_hbm.at[b, e], osem)
        out_copy.start()
        out_copy.wait()


def _ffn(x1, idx, gate, w1, w3, w2, layer, batch, seq_len):
    n, d = x1.shape
    ne, cap = idx.shape[1], idx.shape[2]
    dff = w1.shape[-1]
    fc = 896 if dff % 896 == 0 else dff
    nf = dff // fc
    nblk = batch * ne
    idx3 = idx.reshape(nblk, 1, cap)
    return pl.pallas_call(
        functools.partial(_ffn_kernel, seq_len=seq_len),
        grid=(batch, ne, nf),
        in_specs=[pl.BlockSpec((1, 1, cap), lambda b, e, f: (b * ne + e, 0, 0), memory_space=pltpu.SMEM),
                  pl.BlockSpec((1, 1, cap), lambda b, e, f: (jnp.minimum(b * ne + e + 1, nblk - 1), 0, 0),
                               memory_space=pltpu.SMEM),
                  pl.BlockSpec(memory_space=pl.ANY),
                  pl.BlockSpec((None, None, cap, 1), lambda b, e, f: (b, e, 0, 0)),
                  pl.BlockSpec((None, None, d, fc), lambda b, e, f: (layer, e, 0, f)),
                  pl.BlockSpec((None, None, d, fc), lambda b, e, f: (layer, e, 0, f)),
                  pl.BlockSpec((None, None, fc, d), lambda b, e, f: (layer, e, f, 0))],
        out_specs=pl.BlockSpec(memory_space=pl.ANY),
        out_shape=jax.ShapeDtypeStruct((batch, ne, cap, d), BF16),
        scratch_shapes=[pltpu.VMEM((cap, d), BF16),
                        pltpu.VMEM((cap, d), F32),
                        pltpu.VMEM((cap, d), F32),
                        pltpu.SemaphoreType.DMA(()),
                        pltpu.SemaphoreType.DMA(())],
        compiler_params=_cparams(("arbitrary",) * 3),
        name="expert_ffn",
    )(idx3, idx3, x1, gate, w1, w3, w2)


CMB_WIN = 64
CMB_ALIGN = 16


def _combine_kernel(offs_ref, x_ref, slot_ref, y_hbm, g_ref, b_ref, o_ref, buf_ref, buf2_ref, acc_ref,
                    sem, sem2, *, alpha, cap, tiles_per_seq):
    i = pl.program_id(0)
    nsteps = pl.num_programs(0)
    tc = x_ref.shape[0]
    ne = N_EXPERTS
    kw = ne * CMB_WIN

    def tile(step):
        bidx = step // tiles_per_seq
        return bidx, (bidx * (tiles_per_seq + 1) + step % tiles_per_seq) * ne

    def window(base, e, k):
        start = (offs_ref[base + e] // CMB_ALIGN) * CMB_ALIGN + k * CMB_WIN
        return pl.multiple_of(jnp.minimum(start, cap - CMB_WIN), CMB_ALIGN)

    def first_copy(step, buf, e):
        bidx, base = tile(step)
        return pltpu.make_async_copy(y_hbm.at[bidx, e, pl.ds(window(base, e, 0), CMB_WIN)],
                                     buf_ref.at[buf, pl.ds(e * CMB_WIN, CMB_WIN)], sem.at[buf, e])

    cur = i % 2

    @pl.when(i == 0)
    def _():
        for e in range(ne):
            first_copy(i, cur, e).start()

    @pl.when(i + 1 < nsteps)
    def _():
        for e in range(ne):
            first_copy(i + 1, 1 - cur, e).start()

    bidx, base = tile(i)
    slots = slot_ref[...]
    sp = slots + 1
    el = lax.broadcasted_iota(I32, (ne, kw), 1) // CMB_WIN
    expand = (el == lax.broadcasted_iota(I32, (ne, kw), 0)).astype(BF16)
    sp_exp = (_dot((sp >> 6).astype(F32).astype(BF16), expand) * 64.0
              + _dot((sp & 63).astype(F32).astype(BF16), expand))
    lane1 = lax.broadcasted_iota(I32, (1, kw), 1)
    win_row = jnp.zeros((1, kw), I32)
    for e in range(ne):
        win_row = jnp.where(lane1 // CMB_WIN == e, window(base, e, 0) + 1, win_row)
    oh = (sp_exp - win_row.astype(F32)) == (lane1 % CMB_WIN).astype(F32)
    for e in range(ne):
        first_copy(i, cur, e).wait()
    acc_ref[...] = alpha * x_ref[...] + _dot(oh.astype(BF16), buf_ref[cur])

    lane = lax.broadcasted_iota(I32, (tc, CMB_WIN), 1)
    for e in range(ne):
        off = offs_ref[base + e]
        end = offs_ref[base + ne + e]
        slot = slots[:, e:e + 1]
        n_win = (end - (off // CMB_ALIGN) * CMB_ALIGN + CMB_WIN - 1) // CMB_WIN

        def extra(k, c):
            w0 = window(base, e, k)
            cp = pltpu.make_async_copy(y_hbm.at[bidx, e, pl.ds(w0, CMB_WIN)], buf2_ref, sem2)
            cp.start()
            cp.wait()
            ohk = ((slot - w0) == lane) & ((slot - window(base, e, 0)) >= k * CMB_WIN)
            acc_ref[...] += _dot(ohk.astype(BF16), buf2_ref[...])
            return c

        lax.fori_loop(1, jnp.maximum(n_win, 1), extra, 0)
    o_ref[...] = _layer_norm(acc_ref[...], g_ref[...], b_ref[...])


def _combine(x1, y, slot_tm, offs, g, b, batch, seq_len, alpha):
    n, d = x1.shape
    ne, cap = y.shape[1], y.shape[2]
    tc = min(256, seq_len)
    tps = seq_len // tc
    return pl.pallas_call(
        functools.partial(_combine_kernel, alpha=alpha, cap=cap, tiles_per_seq=tps),
        grid_spec=pltpu.PrefetchScalarGridSpec(
            num_scalar_prefetch=1,
            grid=(n // tc,),
            in_specs=[pl.BlockSpec((tc, d), lambda i, o: (i, 0)),
                      pl.BlockSpec((tc, ne), lambda i, o: (i, 0)),
                      pl.BlockSpec(memory_space=pl.ANY),
                      pl.BlockSpec((1, d), lambda i, o: (0, 0)),
                      pl.BlockSpec((1, d), lambda i, o: (0, 0))],
            out_specs=pl.BlockSpec((tc, d), lambda i, o: (i, 0)),
            scratch_shapes=[pltpu.VMEM((2, ne * CMB_WIN, d), BF16),
                            pltpu.VMEM((CMB_WIN, d), BF16),
                            pltpu.VMEM((tc, d), F32),
                            pltpu.SemaphoreType.DMA((2, ne)),
                            pltpu.SemaphoreType.DMA(())]),
        out_shape=jax.ShapeDtypeStruct((n, d), F32),
        compiler_params=_cparams(("arbitrary",)),
        name="combine_ln",
    )(offs, x1, slot_tm, y, g, b)


def _moe(x1, aff_t, w1, w3, w2, g2, b2, layer, batch, seq_len, alpha):
    cap = EC_FACTOR * seq_len // N_EXPERTS
    idx, gate, slot, offs = _topk(aff_t, cap)
    y = _ffn(x1, idx[..., 0], gate, w1, w3, w2, layer, batch, seq_len)
    tc = min(256, seq_len)
    slot_tm = jnp.transpose(slot.reshape(batch, N_EXPERTS, seq_len), (0, 2, 1)).reshape(batch * seq_len, N_EXPERTS)
    tile_off = offs[:, :, ::tc // LANES, 0]
    tile_off = jnp.concatenate([tile_off, jnp.full((batch, N_EXPERTS, 1), cap, I32)], axis=2)
    tile_off = jnp.transpose(tile_off, (0, 2, 1)).reshape(-1)
    return _combine(x1, y, slot_tm, tile_off, g2, b2, batch, seq_len, alpha)


def kernel(x, positions, w_in, w_out, fno_w, fno_b, lru_conv_w, lru_conv_b, lru_wa, lru_ba, lru_wx, lru_bx,
           lru_lam, dn_conv_w, dn_conv_b, dn_a_log, dn_dt_bias, dn_norm_w, ln1_g, ln1_b, router_w,
           exp_w1, exp_w3, exp_w2, ln2_g, ln2_b):
    batch, seq_len, d = x.shape
    depth = w_in.shape[0]
    gw = GROUP_W
    nh2 = 2 * N_HEADS_G
    alpha = (2.0 * depth) ** 0.25
    n = batch * seq_len

    cos_t, sin_t = _rope_tables(positions)
    tables = _dft_tables(seq_len)

    col = lambda a, b: w_in[:, :, a * gw:b * gw]
    bd = jnp.pad(w_in[:, :, 10 * gw:], ((0, 0), (0, 0), (0, LANES - 2 * nh2)))
    wm = jnp.concatenate([col(0, 1), col(2, 3), col(3, 6), col(9, 10), bd], axis=2).astype(BF16)
    wc = jnp.concatenate([col(1, 2), col(6, 9)], axis=2).astype(BF16)
    cw = jnp.concatenate([lru_conv_w, dn_conv_w], axis=2)
    cb = jnp.concatenate([lru_conv_b, dn_conv_b], axis=1)[:, None, :]
    pad16 = lambda v: jnp.pad(v.reshape(depth, 1, nh2), ((0, 0), (0, 0), (nh2, LANES - 2 * nh2)))
    nalog = pad16(-jnp.exp(dn_a_log.astype(F32)))
    dtb = pad16(dn_dt_bias)
    wf_bd = jax.vmap(_block_diag)(fno_w).astype(BF16)
    bf = fno_b.reshape(depth, 1, gw)
    lru_w = jnp.concatenate([jax.vmap(jax.vmap(_block_diag))(lru_wa),
                             jax.vmap(jax.vmap(_block_diag))(lru_wx)], axis=3).astype(BF16)
    lru_b = jnp.concatenate([lru_ba, lru_bx], axis=2)[:, :, None, :]
    lam = lru_lam[:, :, None, :]
    wo = w_out.astype(BF16)
    nw = jnp.tile(dn_norm_w, (1, N_HEADS_G))[:, None, :]
    rwt = jnp.transpose(router_w, (0, 2, 1)).astype(BF16)

    x2d = x.reshape(n, d)
    for l in range(depth):
        outs = _inproj(x2d, cos_t, sin_t, wm[l], wc[l], cw[l], cb[l], nalog[l], dtb[l], seq_len)
        fa, xc, gg = outs[:3]
        n_att = 3 * len(DIL_PAIRS)
        att_in = outs[3:3 + n_att]
        dq, dk, dv, dsg, dbg = outs[3 + n_att:]
        fy = _fourier(fa, tables, wf_bd[l], bf[l], batch, seq_len)
        yb = _lru(xc, gg, lru_w[l], lru_b[l], lam[l], batch, seq_len)
        att = []
        for p, (window, dil) in enumerate(DIL_PAIRS):
            assert (window // 2) // dil == ATT_R
            att += list(_attn_pattern(*att_in[3 * p:3 * p + 3], dil, batch, seq_len))
        dof, dob = _deltanet(dq, dk, dv, dbg, batch, seq_len)
        x1, aff_t = _outproj(x2d, [fy, yb] + att + [dof, dob, dsg], wo[l], nw[l],
                             ln1_g[l][None, :], ln1_b[l][None, :], rwt[l], batch, seq_len, alpha)
        x2d = _moe(x1, aff_t, exp_w1, exp_w3, exp_w2, ln2_g[l][None, :], ln2_b[l][None, :],
                   l, batch, seq_len, alpha)
    return x2d.reshape(batch, seq_len, d)
```

```python
import functools
import math

import jax
import jax.numpy as jnp
from jax import lax
from jax.experimental import pallas as pl
from jax.experimental.pallas import tpu as pltpu

F32 = jnp.float32
BF16 = jnp.bfloat16
I32 = jnp.int32

GROUP_W = 256
HEAD_DIM = 64
N_HEADS_G = 4
LRU_C = 8.0
DIL_PAIRS = ((128, 1), (512, 4), (2048, 16))
ROPE_THETA = 10000.0
DN_CHUNK = 64
N_EXPERTS = 16
EC_FACTOR = 2
LN_EPS = 1e-5
RMS_EPS = 1e-6
NEG = -1e30

LANES = 128
SUBLANES = 8
VMEM_LIMIT = 56 * 1024 * 1024


def _cparams(sem):
    return pltpu.CompilerParams(dimension_semantics=sem, vmem_limit_bytes=VMEM_LIMIT)


def _dot(a, b):
    return jnp.dot(a, b, preferred_element_type=F32)


def _dot_nt(a, b):
    return lax.dot_general(a, b, (((1,), (1,)), ((), ())), preferred_element_type=F32)


def _bmm(a, b):
    return jnp.einsum('bij,bjk->bik', a, b, preferred_element_type=F32)


def _bmm_nt(a, b):
    return jnp.einsum('bij,bkj->bik', a, b, preferred_element_type=F32)


def _head_sum(t):
    lane = lax.broadcasted_iota(I32, t.shape, 1)
    head = lane // HEAD_DIM
    out = jnp.zeros_like(t)
    for h in range(N_HEADS_G):
        m = head == h
        s = jnp.sum(jnp.where(m, t, 0.0), axis=-1, keepdims=True)
        out = jnp.where(m, s, out)
    return out


def _layer_norm(v, g, b):
    mu = jnp.mean(v, axis=-1, keepdims=True)
    c = v - mu
    var = jnp.mean(c * c, axis=-1, keepdims=True)
    return c * lax.rsqrt(var + LN_EPS) * g + b


def _rope_table_kernel(pos_ref, inv_ref, sgn_ref, cos_ref, sin_ref):
    ang = pos_ref[...] * inv_ref[...]
    cos_ref[...] = jnp.cos(ang)
    sin_ref[...] = jnp.sin(ang) * sgn_ref[...]


def _rope_tables(positions):
    n = positions.size
    half = HEAD_DIM // 2
    inv = ROPE_THETA ** (-jnp.arange(half, dtype=F32) / half)
    inv = jnp.tile(inv, LANES // half)[None, :]
    lane = jnp.arange(LANES)
    sgn = jnp.where((lane % HEAD_DIM) < half, -1.0, 1.0).astype(F32)[None, :]
    pos = positions.reshape(n, 1).astype(F32)
    tm = min(2048, n)
    return pl.pallas_call(
        _rope_table_kernel,
        grid=(n // tm,),
        in_specs=[pl.BlockSpec((tm, 1), lambda i: (i, 0)),
                  pl.BlockSpec((1, LANES), lambda i: (0, 0)),
                  pl.BlockSpec((1, LANES), lambda i: (0, 0))],
        out_specs=[pl.BlockSpec((tm, LANES), lambda i: (i, 0))] * 2,
        out_shape=[jax.ShapeDtypeStruct((n, LANES), F32)] * 2,
        compiler_params=_cparams(("arbitrary",)),
        name="rope_tables",
    )(pos, inv, sgn)


def _inproj_kernel(x_ref, xp_ref, xn_ref, wm_ref, wc_ref, cw_ref, cb_ref, cos_ref, sin_ref,
                   nalog_ref, dtb_ref,
                   fa_ref, xc_ref, gg_ref, *rest, tiles_per_seq):
    n_att = 3 * len(DIL_PAIRS)
    att_refs = rest[:n_att]
    dq_ref, dk_ref, dv_ref, dsg_ref, dbg_ref, scr_ref = rest[n_att:]
    i = pl.program_id(0)
    tm = x_ref.shape[0]
    gw = GROUP_W
    first = (i % tiles_per_seq) == 0
    last = (i % tiles_per_seq) == tiles_per_seq - 1
    xb = x_ref[...].astype(BF16)
    hm = _dot(xb, wm_ref[...])
    hc = _dot(xb, wc_ref[...])
    hp = _dot(xp_ref[...].astype(BF16), wc_ref[...]) * jnp.where(first, 0.0, 1.0)
    hn = _dot(xn_ref[...].astype(BF16), wc_ref[...]) * jnp.where(last, 0.0, 1.0)
    ext = jnp.concatenate([hp, hc, hn], axis=0)
    conv = cb_ref[...]
    for j in range(4):
        conv = conv + cw_ref[j:j + 1, :] * ext[6 + j:6 + j + tm, :]

    fa_ref[...] = hm[:, 0:gw]
    xc_ref[...] = conv[:, 0:gw]
    gg_ref[...] = jax.nn.gelu(hm[:, gw:2 * gw])

    cos2 = jnp.concatenate([cos_ref[...], cos_ref[...]], axis=1)
    sin2 = jnp.concatenate([sin_ref[...], sin_ref[...]], axis=1)
    lane = lax.broadcasted_iota(I32, (tm, gw), 1)
    lo = (lane % HEAD_DIM) < (HEAD_DIM // 2)

    def rope(t):
        rot = jnp.where(lo, pltpu.roll(t, gw - HEAD_DIM // 2, 1), pltpu.roll(t, HEAD_DIM // 2, 1))
        return t * cos2 + rot * sin2

    qkv_att = (rope(hm[:, 2 * gw:3 * gw]) * (HEAD_DIM ** -0.5), rope(hm[:, 3 * gw:4 * gw]), hm[:, 4 * gw:5 * gw])
    nl = gw // LANES
    for j, val in enumerate(qkv_att):
        for c in range(nl):
            scr_ref[j, c] = val[:, c * LANES:(c + 1) * LANES]
    for p, (_, dil) in enumerate(DIL_PAIRS):
        for j in range(3):
            ref = att_refs[3 * p + j]
            if dil == 1:
                ref[...] = qkv_att[j].astype(BF16)
            else:
                for r in range(dil):
                    for c in range(nl):
                        c0 = r * gw + c * LANES
                        ref[:, c0:c0 + LANES] = scr_ref[j, c, pl.ds(r, tm // dil, stride=dil), :].astype(BF16)

    qkv = jax.nn.silu(conv[:, gw:])

    def l2n(t):
        return t * lax.rsqrt(_head_sum(t * t) + RMS_EPS)

    dq_ref[...] = l2n(qkv[:, 0:gw])
    dk_ref[...] = l2n(qkv[:, gw:2 * gw])
    dv_ref[...] = qkv[:, 2 * gw:3 * gw]
    dsg_ref[...] = jax.nn.silu(hm[:, 5 * gw:6 * gw])

    bd = hm[:, 6 * gw:6 * gw + LANES]
    l128 = lax.broadcasted_iota(I32, (tm, LANES), 1)
    beta = jax.nn.sigmoid(bd)
    g = nalog_ref[...] * jax.nn.softplus(bd + dtb_ref[...])
    nh2 = 2 * N_HEADS_G
    dbg_ref[...] = jnp.where(l128 < nh2, beta, jnp.where(l128 < 2 * nh2, g, 0.0))


def _inproj(x2d, cos_t, sin_t, wm, wc, cw, cb, nalog, dtb, seq_len):
    n, d = x2d.shape
    tm = min(512, seq_len)
    nt = n // tm
    tps = seq_len // tm
    hb = tm // SUBLANES
    gw = GROUP_W
    full = lambda shape: pl.BlockSpec(shape, lambda i: (0,) * len(shape))
    row = lambda w: pl.BlockSpec((tm, w), lambda i: (i, 0))
    att_specs, att_shapes = [], []
    for _, dil in DIL_PAIRS:
        att_specs += [pl.BlockSpec((tm // dil, dil * gw), lambda i: (i, 0))] * 3
        att_shapes += [jax.ShapeDtypeStruct((n // dil, dil * gw), BF16)] * 3
    f32_out = lambda w: jax.ShapeDtypeStruct((n, w), F32)
    out_specs = [row(gw)] * 3 + att_specs + [row(gw)] * 4 + [row(LANES)]
    out_shape = [f32_out(gw)] * 3 + att_shapes + [f32_out(gw)] * 4 + [f32_out(LANES)]
    return pl.pallas_call(
        functools.partial(_inproj_kernel, tiles_per_seq=tps),
        grid=(nt,),
        in_specs=[row(d),
                  pl.BlockSpec((SUBLANES, d), lambda i: (jnp.maximum(i * hb - 1, 0), 0)),
                  pl.BlockSpec((SUBLANES, d), lambda i: (jnp.minimum((i + 1) * hb, nt * hb - 1), 0)),
                  full(wm.shape), full(wc.shape), full(cw.shape), full(cb.shape),
                  row(LANES), row(LANES), full(nalog.shape), full(dtb.shape)],
        out_specs=out_specs,
        out_shape=out_shape,
        scratch_shapes=[pltpu.VMEM((3, gw // LANES, tm, LANES), F32)],
        compiler_params=_cparams(("arbitrary",)),
        name="inproj",
    )(x2d, x2d, x2d, wm, wc, cw, cb, cos_t, sin_t, nalog, dtb)


def _fourier1_kernel(x_ref, t_ref, o_ref):
    for j in range(SUBLANES):
        xj = x_ref[:, j, :].astype(BF16)
        o_ref[j] = _dot(t_ref[j], xj)


def _fourier2_kernel(re_ref, im_ref, w2_ref, cc_ref, sc_ref, wf_ref, bf_ref, o_ref, *, scale):
    n2 = re_ref.shape[0]
    for j in range(SUBLANES):
        a = jnp.concatenate([re_ref[:, j, :], im_ref[:, j, :]], axis=0).astype(BF16)
        z = _dot(w2_ref[...], a)
        zr = z[0:n2].astype(BF16)
        zi = z[n2:2 * n2].astype(BF16)
        f = (_dot(zr, cc_ref[...]) + _dot(zi, sc_ref[...])) * scale
        o_ref[:, j, :] = _dot(f.astype(BF16), wf_ref[...]) + bf_ref[...]


def _dft_tables(seq_len):
    n1 = 1 << (int(math.log2(seq_len)) // 2)
    n2 = seq_len // n1
    k1 = jnp.arange(n1, dtype=I32)
    s1 = jnp.arange(n1, dtype=I32)
    s2 = jnp.arange(n2, dtype=I32)
    m = (k1[None, :, None] * (s1[None, None, :] * n2 + s2[:, None, None])) % seq_len
    ang = m.astype(F32) * (2.0 * math.pi / seq_len)
    t1 = jnp.concatenate([jnp.cos(ang), -jnp.sin(ang)], axis=1).astype(BF16)
    k2 = jnp.arange(n2, dtype=I32)
    m2 = (k2[:, None] * s2[None, :]) % n2
    a2 = m2.astype(F32) * (2.0 * math.pi / n2)
    c2, sn2 = jnp.cos(a2), jnp.sin(a2)
    w2 = jnp.concatenate([jnp.concatenate([c2, sn2], axis=1),
                          jnp.concatenate([-sn2, c2], axis=1)], axis=0).astype(BF16)
    c = jnp.arange(HEAD_DIM, dtype=I32)
    ac = ((c[:, None] * c[None, :]) % HEAD_DIM).astype(F32) * (2.0 * math.pi / HEAD_DIM)
    eye = jnp.eye(N_HEADS_G, dtype=F32)
    cc = jnp.kron(eye, jnp.cos(ac)).astype(BF16)
    sc = jnp.kron(eye, jnp.sin(ac)).astype(BF16)
    return n1, n2, t1, w2, cc, sc


def _block_diag(w):
    h, a, b = w.shape
    eye = jnp.eye(h, dtype=w.dtype)
    return (eye[:, None, :, None] * w[:, :, None, :]).reshape(h * a, h * b)


def _fourier(fa, tables, wf_bd, bf, batch, seq_len):
    n1, n2, t1, w2, cc, sc = tables
    gw = GROUP_W
    x4 = fa.reshape(batch, n1, n2, gw)
    a = pl.pallas_call(
        _fourier1_kernel,
        grid=(batch, n2 // SUBLANES),
        in_specs=[pl.BlockSpec((None, n1, SUBLANES, gw), lambda b, i: (b, 0, i, 0)),
                  pl.BlockSpec((SUBLANES, 2 * n1, n1), lambda b, i: (i, 0, 0))],
        out_specs=pl.BlockSpec((None, SUBLANES, 2 * n1, gw), lambda b, i: (b, i, 0, 0)),
        out_shape=jax.ShapeDtypeStruct((batch, n2, 2 * n1, gw), F32),
        compiler_params=_cparams(("arbitrary", "arbitrary")),
        name="fourier_stage1",
    )(x4, t1)
    nb = n1 // SUBLANES
    full = lambda arr: pl.BlockSpec(arr.shape, lambda b, i: (0,) * arr.ndim)
    y = pl.pallas_call(
        functools.partial(_fourier2_kernel, scale=1.0 / math.sqrt(seq_len * HEAD_DIM)),
        grid=(batch, nb),
        in_specs=[pl.BlockSpec((None, n2, SUBLANES, gw), lambda b, i: (b, 0, i, 0)),
                  pl.BlockSpec((None, n2, SUBLANES, gw), lambda b, i: (b, 0, nb + i, 0)),
                  full(w2), full(cc), full(sc), full(wf_bd), full(bf)],
        out_specs=pl.BlockSpec((None, n2, SUBLANES, gw), lambda b, i: (b, 0, i, 0)),
        out_shape=jax.ShapeDtypeStruct((batch, n2, n1, gw), F32),
        compiler_params=_cparams(("arbitrary", "arbitrary")),
        name="fourier_stage2",
    )(a, a, w2, cc, sc, wf_bd, bf)
    return y.reshape(batch * seq_len, gw)


def _lru_kernel(*refs, rev):
    if rev:
        xc_ref, w_ref, b_ref, lam_ref, hf_ref, gg_ref, o_ref, carry_ref = refs
    else:
        xc_ref, w_ref, b_ref, lam_ref, o_ref, carry_ref = refs
    gw = GROUP_W
    t = xc_ref.shape[0]

    @pl.when(pl.program_id(1) == 0)
    def _():
        carry_ref[...] = jnp.zeros_like(carry_ref)

    xc = xc_ref[...]
    gates = _dot(xc.astype(BF16), w_ref[...]) + b_ref[...]
    r = jax.nn.sigmoid(gates[:, 0:gw])
    ig = jax.nn.sigmoid(gates[:, gw:2 * gw])
    log_a = -LRU_C * r * jax.nn.softplus(-lam_ref[...])
    a = jnp.exp(log_a)
    th = jnp.tanh(log_a)
    u = jnp.sqrt(-2.0 * th / (1.0 - th)) * (ig * xc)

    r8 = lax.broadcasted_iota(I32, (t, gw), 0) % SUBLANES
    for s in (1, 2, 4):
        if rev:
            a_s, u_s, m = pltpu.roll(a, t - s, 0), pltpu.roll(u, t - s, 0), r8 < SUBLANES - s
        else:
            a_s, u_s, m = pltpu.roll(a, s, 0), pltpu.roll(u, s, 0), r8 >= s
        u = jnp.where(m, a * u_s + u, u)
        a = jnp.where(m, a * a_s, a)
    carry = carry_ref[...]
    ng = t // SUBLANES
    blocks = [None] * ng
    for g in (range(ng - 1, -1, -1) if rev else range(ng)):
        sl = slice(g * SUBLANES, (g + 1) * SUBLANES)
        blk = u[sl] + a[sl] * carry
        carry = blk[0:1] if rev else blk[SUBLANES - 1:SUBLANES]
        blocks[g] = blk
    carry_ref[...] = carry
    h = jnp.concatenate(blocks, axis=0)
    if rev:
        o_ref[...] = (hf_ref[...] + h) * gg_ref[...]
    else:
        o_ref[...] = h


def _lru(xc, gg, w_gate, b_gate, lam, batch, seq_len):
    n, gw = xc.shape
    t = min(512, seq_len)
    nt = seq_len // t
    fwd = lambda b, i: (b * nt + i, 0)
    bwd = lambda b, i: (b * nt + (nt - 1 - i), 0)
    outs = None
    for d, imap in ((0, fwd), (1, bwd)):
        par = lambda arr: pl.BlockSpec((None,) + arr.shape[1:], lambda b, i, d=d: (d,) + (0,) * (arr.ndim - 1))
        tile = pl.BlockSpec((t, gw), imap)
        ins = [xc, w_gate, b_gate, lam]
        specs = [tile, par(w_gate), par(b_gate), par(lam)]
        if d == 1:
            ins += [outs, gg]
            specs += [tile, tile]
        outs = pl.pallas_call(
            functools.partial(_lru_kernel, rev=bool(d)),
            grid=(batch, nt),
            in_specs=specs,
            out_specs=tile,
            out_shape=jax.ShapeDtypeStruct((n, gw), F32),
            scratch_shapes=[pltpu.VMEM((1, gw), F32)],
            compiler_params=_cparams(("arbitrary", "arbitrary")),
            name="lru_bwd" if d else "lru_fwd",
        )(*ins)
    return outs


ATT_R = 64
ATT_QB = 128


def _attn_kernel(q_ref, kp_ref, k_ref, kn_ref, vp_ref, v_ref, vn_ref, o_ref, l_ref, *, sub_len):
    tq = q_ref.shape[0]
    base = pl.program_id(2) * tq
    kext = jnp.concatenate([kp_ref[...], k_ref[...], kn_ref[...]], axis=0)
    vext = jnp.concatenate([vp_ref[...], v_ref[...], vn_ref[...]], axis=0)
    qb = min(ATT_QB, tq)
    kw = qb + 2 * ATT_R
    nh = N_HEADS_G
    nq = tq // qb
    order = [(jq, h) for jq in range(nq) for h in range(nh)]
    hs = lambda h: slice(h * HEAD_DIM, (h + 1) * HEAD_DIM)
    q3 = jnp.stack([q_ref[jq * qb:(jq + 1) * qb, hs(h)] for jq, h in order])
    k3 = jnp.stack([kext[jq * qb:jq * qb + kw, hs(h)] for jq, h in order])
    v3 = jnp.stack([vext[jq * qb:jq * qb + kw, hs(h)] for jq, h in order])
    shape = (nq * nh, qb, kw)
    qi = lax.broadcasted_iota(I32, shape, 1)
    kj = lax.broadcasted_iota(I32, shape, 2)
    jqi = lax.broadcasted_iota(I32, shape, 0) // nh
    kpos = base + jqi * qb + kj - ATT_R
    mask = (jnp.abs(kj - ATT_R - qi) <= ATT_R) & (kpos >= 0) & (kpos < sub_len)
    s = jnp.where(mask, _bmm_nt(q3, k3), NEG)
    m = jnp.max(s, axis=-1, keepdims=True)
    p = jnp.exp(s - m)
    l = jnp.sum(p, axis=-1, keepdims=True)
    o3 = _bmm(p.astype(BF16), v3) / l
    lse3 = jnp.broadcast_to(m + jnp.log(l), (nq * nh, qb, HEAD_DIM))
    for jq in range(nq):
        o_ref[jq * qb:(jq + 1) * qb, :] = jnp.concatenate([o3[jq * nh + h] for h in range(nh)], axis=1)
        l_ref[jq * qb:(jq + 1) * qb, :] = jnp.concatenate([lse3[jq * nh + h] for h in range(nh)], axis=1)


def _attn_pattern(aq, ak, av, dil, batch, seq_len):
    gw = GROUP_W
    sub = seq_len // dil
    q3, k3, v3 = (t.reshape(batch, sub, dil * gw) for t in (aq, ak, av))
    tq = min(512, sub)
    nq = sub // tq
    hb = tq // ATT_R
    nhb = sub // ATT_R
    main = pl.BlockSpec((None, tq, gw), lambda b, r, i: (b, i, r))
    prev = pl.BlockSpec((None, ATT_R, gw), lambda b, r, i: (b, jnp.maximum(i * hb - 1, 0), r))
    nxt = pl.BlockSpec((None, ATT_R, gw), lambda b, r, i: (b, jnp.minimum((i + 1) * hb, nhb - 1), r))
    o, l = pl.pallas_call(
        functools.partial(_attn_kernel, sub_len=sub),
        grid=(batch, dil, nq),
        in_specs=[main, prev, main, nxt, prev, main, nxt],
        out_specs=[main, main],
        out_shape=[jax.ShapeDtypeStruct((batch, sub, dil * gw), F32)] * 2,
        compiler_params=_cparams(("arbitrary", "arbitrary", "arbitrary")),
        name=f"dilated_attn_d{dil}",
    )(q3, k3, k3, k3, v3, v3, v3)
    return o.reshape(batch * sub, dil * gw), l.reshape(batch * sub, dil * gw)


DN_TILE = 256


def _deltanet_kernel(qf_ref, kf_ref, vf_ref, gf_ref, qb_ref, kb_ref, vb_ref, gb_ref,
                     of_ref, ob_ref, state_ref):
    nbat, t = qf_ref.shape[0], qf_ref.shape[1]
    cl = DN_CHUNK
    nc = t // cl
    nh = N_HEADS_G
    hd = HEAD_DIM
    gw = GROUP_W
    nb = nbat * 2 * nc

    @pl.when(pl.program_id(0) == 0)
    def _():
        state_ref[...] = jnp.zeros_like(state_ref)

    rc = lax.broadcasted_iota(I32, (t, LANES), 0) % cl
    gcs, gts = [], []
    for d, g_ref in ((0, gf_ref), (1, gb_ref)):
        gcl, gtl = [], []
        for bi in range(nbat):
            gc = g_ref[bi]
            s = 1
            while s < cl:
                if d:
                    gc = jnp.where(rc < cl - s, gc + pltpu.roll(gc, t - s, 0), gc)
                else:
                    gc = jnp.where(rc >= s, gc + pltpu.roll(gc, s, 0), gc)
                s *= 2
            gcl.append(gc)
            gtl.append(gc.T)
        gcs.append(gcl)
        gts.append(gtl)

    order = [(bi, d, c) for bi in range(nbat) for d in range(2) for c in range(nc)]
    data = ((qf_ref, kf_ref, vf_ref, gf_ref), (qb_ref, kb_ref, vb_ref, gb_ref))
    rows = lambda c: slice(c * cl, (c + 1) * cl)

    q3 = jnp.stack([data[d][0][bi, rows(c), :] for bi, d, c in order])
    k3 = jnp.stack([data[d][1][bi, rows(c), :] for bi, d, c in order])
    v3 = jnp.stack([data[d][2][bi, rows(c), :] for bi, d, c in order])

    def lane_expand(src_of, col_of):
        blocks = []
        for bi, d, c in order:
            src = src_of(d, bi)[rows(c), :]
            blocks.append(jnp.concatenate(
                [jnp.broadcast_to(src[:, col_of(d, h):col_of(d, h) + 1], (cl, hd)) for h in range(nh)], axis=1))
        return jnp.stack(blocks)

    g_col = lambda d, h: 2 * nh + d * nh + h
    beta3 = lane_expand(lambda d, bi: data[d][3][bi], lambda d, h: d * nh + h)
    gcc3 = lane_expand(lambda d, bi: gcs[d][bi], g_col)
    gcr3 = jnp.stack([jnp.concatenate([gts[d][bi][g_col(d, h):g_col(d, h) + 1, rows(c)] for h in range(nh)], axis=1)
                      for bi, d, c in order])

    ri = lax.broadcasted_iota(I32, (nb, cl, gw), 1)
    ci = lax.broadcasted_iota(I32, (nb, cl, gw), 2) % hd
    isb = (lax.broadcasted_iota(I32, (nb, cl, gw), 0) // nc) % 2 == 1
    r2 = jnp.where(isb, ci, ri)
    c2 = jnp.where(isb, ri, ci)
    incl = r2 >= c2
    strict = r2 > c2
    eye = (ri == ci).astype(F32)

    def block_diag(x):
        n = x.shape[0]
        same_head = (lax.broadcasted_iota(I32, (n, gw, gw), 1) // hd) == (lax.broadcasted_iota(I32, (n, gw, gw), 2) // hd)
        return jnp.where(same_head, jnp.concatenate([x.astype(BF16)] * nh, axis=1), jnp.zeros((), BF16))

    decay = jnp.where(incl, jnp.exp(jnp.where(incl, gcc3 - gcr3, 0.0)), 0.0)
    kb3 = k3 * beta3
    vb3 = v3 * beta3
    qs = q3 * (hd ** -0.5)
    mk = _bmm_nt(jnp.concatenate([kb3, qs], axis=1).astype(BF16), block_diag(k3))
    m = mk[:, 0:cl, :] * jnp.where(strict, decay, 0.0)
    a3 = (mk[:, cl:2 * cl, :] * decay).astype(BF16)
    blk = lambda s: (ri // s) == (ci // s)
    x4 = jnp.where(blk(4), -m, 0.0)
    x4sq = _bmm(x4.astype(BF16), block_diag(x4))
    tm = eye + x4
    tm = tm + _bmm(tm.astype(BF16), block_diag(x4sq))
    for s in (4, 8, 16, 32):
        cm = jnp.where(blk(2 * s) & jnp.logical_not(blk(s)), m, 0.0)
        tc_ = _bmm(cm.astype(BF16), block_diag(tm))
        tm = tm - _bmm(tm.astype(BF16), block_diag(tc_))
    eg = jnp.exp(gcc3)
    tmb = tm.astype(BF16)
    w3 = _bmm(tmb, block_diag(vb3))
    u3 = _bmm(tmb, block_diag(kb3 * eg)).astype(BF16)
    qd3 = (qs * eg).astype(BF16)
    isb1 = (lax.broadcasted_iota(I32, (nb, 1, gw), 0) // nc) % 2 == 1
    gl3 = jnp.where(isb1, gcc3[:, 0:1, :], gcc3[:, cl - 1:cl, :])
    kd3 = k3 * jnp.exp(gl3 - gcc3)
    kdt3 = _bmm_nt(eye.astype(BF16), block_diag(kd3)).astype(BF16)
    egl3 = jnp.exp(gl3)
    uq3 = jnp.concatenate([u3, qd3], axis=1)
    ak3 = jnp.concatenate([a3, kdt3], axis=1)

    def step_bodies(arr, c):
        parts = []
        for bi in range(nbat):
            f0 = bi * 2 * nc + c
            b0 = bi * 2 * nc + nc + (nc - 1 - c)
            parts += [arr[f0:f0 + 1], arr[b0:b0 + 1]]
        return jnp.concatenate(parts, axis=0)

    st = state_ref[...]
    for c in range(nc):
        us = _bmm(step_bodies(uq3, c), block_diag(st))
        v_new = step_bodies(w3, c) - us[:, 0:cl, :]
        av = _bmm(step_bodies(ak3, c), block_diag(v_new))
        o2 = us[:, cl:2 * cl, :] + av[:, 0:cl, :]
        st = st * step_bodies(egl3, c) + av[:, cl:2 * cl, :]
        for bi in range(nbat):
            of_ref[bi, rows(c), :] = o2[2 * bi]
            ob_ref[bi, rows(nc - 1 - c), :] = o2[2 * bi + 1]
    state_ref[...] = st


def _deltanet(dq, dk, dv, dbg, batch, seq_len):
    n, gw = dq.shape
    t = min(DN_TILE, seq_len)
    nt = seq_len // t
    fwd = lambda i: (0, i, 0)
    bwd = lambda i: (0, nt - 1 - i, 0)
    tf, tb = pl.BlockSpec((batch, t, gw), fwd), pl.BlockSpec((batch, t, gw), bwd)
    gf, gb = pl.BlockSpec((batch, t, LANES), fwd), pl.BlockSpec((batch, t, LANES), bwd)
    dq, dk, dv = (a.reshape(batch, seq_len, gw) for a in (dq, dk, dv))
    dbg = dbg.reshape(batch, seq_len, LANES)
    outs = pl.pallas_call(
        _deltanet_kernel,
        grid=(nt,),
        in_specs=[tf, tf, tf, gf, tb, tb, tb, gb],
        out_specs=[tf, tb],
        out_shape=[jax.ShapeDtypeStruct((batch, seq_len, gw), F32)] * 2,
        scratch_shapes=[pltpu.VMEM((2 * batch, HEAD_DIM, GROUP_W), F32)],
        compiler_params=_cparams(("arbitrary",)),
        name="deltanet",
    )(dq, dk, dv, dbg, dq, dk, dv, dbg)
    return [o.reshape(n, gw) for o in outs]


def _outproj_kernel(x_ref, fy_ref, yb_ref, o1_ref, l1_ref, o2_ref, l2_ref, o3_ref, l3_ref,
                    dof_ref, dob_ref, dsg_ref, wo_ref, nw_ref, g_ref, b_ref, rwt_ref,
                    x1_ref, aff_ref, scr_ref, *, alpha):
    gw = GROUP_W
    tm = x_ref.shape[0]
    att = []
    for p, ((_, dil), refs) in enumerate(zip(DIL_PAIRS, ((o1_ref, l1_ref), (o2_ref, l2_ref), (o3_ref, l3_ref)))):
        for j, ref in enumerate(refs):
            if dil == 1:
                att.append(ref[...])
            else:
                nl = gw // LANES
                for r in range(dil):
                    for c in range(nl):
                        c0 = r * gw + c * LANES
                        scr_ref[2 * p + j, c, pl.ds(r, tm // dil, stride=dil), :] = ref[:, c0:c0 + LANES]
                att.append(jnp.concatenate([scr_ref[2 * p + j, c] for c in range(nl)], axis=1))
    o1, l1, o2, l2, o3, l3 = att
    m = jnp.maximum(jnp.maximum(l1, l2), l3)
    w1, w2, w3 = jnp.exp(l1 - m), jnp.exp(l2 - m), jnp.exp(l3 - m)
    yc = (w1 * o1 + w2 * o2 + w3 * o3) / (w1 + w2 + w3)
    od = dof_ref[...] + dob_ref[...]
    ms = _head_sum(od * od) * (1.0 / HEAD_DIM)
    yd = od * lax.rsqrt(ms + RMS_EPS) * nw_ref[...] * dsg_ref[...]
    y = (_dot(fy_ref[...].astype(BF16), wo_ref[0:gw, :])
         + _dot(yb_ref[...].astype(BF16), wo_ref[gw:2 * gw, :])
         + _dot(yc.astype(BF16), wo_ref[2 * gw:3 * gw, :])
         + _dot(yd.astype(BF16), wo_ref[3 * gw:4 * gw, :]))
    x1 = _layer_norm(alpha * x_ref[...] + y, g_ref[...], b_ref[...])
    x1_ref[...] = x1
    logits = _dot_nt(rwt_ref[...], x1.astype(BF16))
    mx = jnp.max(logits, axis=0, keepdims=True)
    e = jnp.exp(logits - mx)
    aff_ref[...] = e / jnp.sum(e, axis=0, keepdims=True)


def _outproj(x2d, parts, wo, nw, g, b, rwt, batch, seq_len, alpha):
    n, d = x2d.shape
    gw = GROUP_W
    tm = min(512, seq_len)
    tps = seq_len // tm
    row = lambda w: pl.BlockSpec((tm, w), lambda i: (i, 0))
    full = lambda arr: pl.BlockSpec(arr.shape, lambda i: (0,) * arr.ndim)
    att_specs = []
    for _, dil in DIL_PAIRS:
        att_specs += [pl.BlockSpec((tm // dil, dil * gw), lambda i: (i, 0))] * 2
    return pl.pallas_call(
        functools.partial(_outproj_kernel, alpha=alpha),
        grid=(n // tm,),
        in_specs=[row(d)] + [row(gw)] * 2 + att_specs + [row(gw)] * 3
                 + [full(wo), full(nw), full(g), full(b), full(rwt)],
        out_specs=[row(d), pl.BlockSpec((None, N_EXPERTS, tm), lambda i: (i // tps, 0, i % tps))],
        out_shape=[jax.ShapeDtypeStruct((n, d), F32),
                   jax.ShapeDtypeStruct((batch, N_EXPERTS, seq_len), F32)],
        scratch_shapes=[pltpu.VMEM((2 * len(DIL_PAIRS), gw // LANES, tm, LANES), F32)],
        compiler_params=_cparams(("arbitrary",)),
        name="outproj_ln_router",
    )(x2d, *parts, wo, nw, g, b, rwt)


def _topk_kernel(a_ref, idx_ref, gate_ref, slot_ref, off_ref, *, cap):
    ng = a_ref.shape[0]
    v = a_ref[...]
    bits = pltpu.bitcast(v, I32)

    thr = jnp.int32(0)
    for shift in range(28, -1, -4):
        n_cand = 7 if shift == 28 else 15
        digit = jnp.int32(0)
        for c in range(1, n_cand + 1):
            cnt = jnp.sum((bits >= (thr | jnp.int32(c << shift))).astype(I32))
            digit = digit + (cnt >= cap).astype(I32)
        thr = thr | (digit << shift)
    gt = bits > thr
    eq = bits == thr
    need_eq = cap - jnp.sum(gt.astype(I32))

    li = lax.broadcasted_iota(I32, (LANES, LANES), 0)
    lj = lax.broadcasted_iota(I32, (LANES, LANES), 1)
    ut_incl = (li <= lj).astype(BF16)
    gi = lax.broadcasted_iota(I32, (ng, ng), 0)
    gj = lax.broadcasted_iota(I32, (ng, ng), 1)
    lt_strict = (gj < gi).astype(BF16)

    def prefix(mask):
        p1 = _dot(mask.astype(BF16), ut_incl)
        tot = jnp.broadcast_to(p1[:, LANES - 1:LANES], (ng, LANES))
        return p1, _dot(lt_strict, tot.astype(BF16)), tot

    p1e, offe, _ = prefix(eq)
    rank_eq = p1e - eq.astype(F32) + offe
    sel = gt | (eq & (rank_eq < need_eq.astype(F32)))
    p1, offs, tot = prefix(sel)
    slot_ref[...] = jnp.where(sel, (p1 + offs).astype(I32) - 1, -1)
    off_ref[...] = offs.astype(I32)

    j = lax.broadcasted_iota(I32, (cap, 1), 0).astype(F32)
    ends_row = (offs + tot).T[0:1, :]
    offs_row = offs.T[0:1, :]
    gj_ = jnp.sum((ends_row <= j).astype(I32), axis=-1, keepdims=True)
    oh = lax.broadcasted_iota(I32, (cap, ng), 1) == gj_
    off_j = jnp.sum(jnp.where(oh, offs_row, 0.0), axis=-1, keepdims=True)
    ohb = oh.astype(BF16)
    prow = _dot(ohb, p1.astype(BF16))
    lo = jnp.sum((prow <= (j - off_j)).astype(I32), axis=-1, keepdims=True)
    idx_ref[...] = gj_ * LANES + lo
    h1 = v.astype(BF16)
    r1 = v - h1.astype(F32)
    h2 = r1.astype(BF16)
    h3 = (r1 - h2.astype(F32)).astype(BF16)
    arow = _dot(ohb, h1) + _dot(ohb, h2) + _dot(ohb, h3)
    lane = lax.broadcasted_iota(I32, (cap, LANES), 1)
    gate_ref[...] = jnp.sum(jnp.where(lane == lo, arow, 0.0), axis=-1, keepdims=True)


def _topk(aff_t, cap):
    batch, ne, seq_len = aff_t.shape
    ng = seq_len // LANES
    a4 = aff_t.reshape(batch, ne, ng, LANES)
    grp = pl.BlockSpec((None, None, ng, LANES), lambda b, e: (b, e, 0, 0))
    col = pl.BlockSpec((None, None, cap, 1), lambda b, e: (b, e, 0, 0))
    return pl.pallas_call(
        functools.partial(_topk_kernel, cap=cap),
        grid=(batch, ne),
        in_specs=[grp],
        out_specs=[col, col, grp, grp],
        out_shape=[jax.ShapeDtypeStruct((batch, ne, cap, 1), I32),
                   jax.ShapeDtypeStruct((batch, ne, cap, 1), F32),
                   jax.ShapeDtypeStruct((batch, ne, ng, LANES), I32),
                   jax.ShapeDtypeStruct((batch, ne, ng, LANES), I32)],
        compiler_params=_cparams(("arbitrary", "arbitrary")),
        name="expert_topk",
    )(a4)


FFN_ROWS = 512


def _ffn_kernel(idx_ref, idxn_ref, x_hbm, gate_ref, w1_ref, w3_ref, w2_ref, y_hbm,
                xg_ref, gbuf_ref, acc_ref, sem, osem, *, seq_len):
    b = pl.program_id(0)
    e = pl.program_id(1)
    f = pl.program_id(2)
    ne = pl.num_programs(1)
    blk_id = b * ne + e
    n_blocks = pl.num_programs(0) * ne
    cap = xg_ref.shape[0]
    rb = min(FFN_ROWS, cap)

    def row_copy(iref, base, r):
        return pltpu.make_async_copy(x_hbm.at[pl.ds(base + iref[0, 0, r], 1)], gbuf_ref.at[pl.ds(r, 1)], sem)

    def gather_start(iref, base):
        def body(r, c):
            row_copy(iref, base, r).start()
            return c
        lax.fori_loop(0, cap, body, 0, unroll=8)

    @pl.when(f == 0)
    def _():
        @pl.when(blk_id == 0)
        def _():
            gather_start(idx_ref, b * seq_len)

        def wait_body(r, c):
            row_copy(idx_ref, b * seq_len, r).wait()
            return c
        lax.fori_loop(0, cap, wait_body, 0, unroll=8)
        for blk in range(cap // rb):
            rs = slice(blk * rb, (blk + 1) * rb)
            xg_ref[rs, :] = gbuf_ref[rs, :].astype(BF16)
            acc_ref[rs, :] = jnp.zeros((rb, acc_ref.shape[1]), F32)

        @pl.when(blk_id + 1 < n_blocks)
        def _():
            gather_start(idxn_ref, ((blk_id + 1) // ne) * seq_len)

    fc = w1_ref.shape[1]
    w13 = jnp.concatenate([w1_ref[...].astype(BF16), w3_ref[...].astype(BF16)], axis=1)
    w2 = w2_ref[...].astype(BF16)
    for blk in range(cap // rb):
        rs = slice(blk * rb, (blk + 1) * rb)
        h13 = _dot(xg_ref[rs, :], w13)
        h = (jax.nn.silu(h13[:, 0:fc]) * h13[:, fc:2 * fc]).astype(BF16)
        acc_ref[rs, :] += _dot(h, w2)

    @pl.when(f == pl.num_programs(2) - 1)
    def _():
        for blk in range(cap // rb):
            rs = slice(blk * rb, (blk + 1) * rb)
            xg_ref[rs, :] = (acc_ref[rs, :] * gate_ref[rs, :]).astype(BF16)
        out_copy = pltpu.make_async_copy(xg_ref, y_hbm.at[b, e], osem)
        out_copy.start()
        out_copy.wait()


def _ffn(x1, idx, gate, w1, w3, w2, layer, batch, seq_len):
    n, d = x1.shape
    ne, cap = idx.shape[1], idx.shape[2]
    dff = w1.shape[-1]
    fc = 896 if dff % 896 == 0 else dff
    nf = dff // fc
    nblk = batch * ne
    idx3 = idx.reshape(nblk, 1, cap)
    return pl.pallas_call(
        functools.partial(_ffn_kernel, seq_len=seq_len),
        grid=(batch, ne, nf),
        in_specs=[pl.BlockSpec((1, 1, cap), lambda b, e, f: (b * ne + e, 0, 0), memory_space=pltpu.SMEM),
                  pl.BlockSpec((1, 1, cap), lambda b, e, f: (jnp.minimum(b * ne + e + 1, nblk - 1), 0, 0),
                               memory_space=pltpu.SMEM),
                  pl.BlockSpec(memory_space=pl.ANY),
                  pl.BlockSpec((None, None, cap, 1), lambda b, e, f: (b, e, 0, 0)),
                  pl.BlockSpec((None, None, d, fc), lambda b, e, f: (layer, e, 0, f)),
                  pl.BlockSpec((None, None, d, fc), lambda b, e, f: (layer, e, 0, f)),
                  pl.BlockSpec((None, None, fc, d), lambda b, e, f: (layer, e, f, 0))],
        out_specs=pl.BlockSpec(memory_space=pl.ANY),
        out_shape=jax.ShapeDtypeStruct((batch, ne, cap, d), BF16),
        scratch_shapes=[pltpu.VMEM((cap, d), BF16),
                        pltpu.VMEM((cap, d), F32),
                        pltpu.VMEM((cap, d), F32),
                        pltpu.SemaphoreType.DMA(()),
                        pltpu.SemaphoreType.DMA(())],
        compiler_params=_cparams(("arbitrary",) * 3),
        name="expert_ffn",
    )(idx3, idx3, x1, gate, w1, w3, w2)


CMB_WIN = 64
CMB_ALIGN = 16


def _combine_kernel(offs_ref, x_ref, slot_ref, y_hbm, g_ref, b_ref, o_ref, buf_ref, buf2_ref, acc_ref,
                    sem, sem2, *, alpha, cap, tiles_per_seq):
    i = pl.program_id(0)
    nsteps = pl.num_programs(0)
    tc = x_ref.shape[0]
    ne = N_EXPERTS
    kw = ne * CMB_WIN

    def tile(step):
        bidx = step // tiles_per_seq
        return bidx, (bidx * (tiles_per_seq + 1) + step % tiles_per_seq) * ne

    def window(base, e, k):
        start = (offs_ref[base + e] // CMB_ALIGN) * CMB_ALIGN + k * CMB_WIN
        return pl.multiple_of(jnp.minimum(start, cap - CMB_WIN), CMB_ALIGN)

    def first_copy(step, buf, e):
        bidx, base = tile(step)
        return pltpu.make_async_copy(y_hbm.at[bidx, e, pl.ds(window(base, e, 0), CMB_WIN)],
                                     buf_ref.at[buf, pl.ds(e * CMB_WIN, CMB_WIN)], sem.at[buf, e])

    cur = i % 2

    @pl.when(i == 0)
    def _():
        for e in range(ne):
            first_copy(i, cur, e).start()

    @pl.when(i + 1 < nsteps)
    def _():
        for e in range(ne):
            first_copy(i + 1, 1 - cur, e).start()

    bidx, base = tile(i)
    slots = slot_ref[...]
    sp = slots + 1
    el = lax.broadcasted_iota(I32, (ne, kw), 1) // CMB_WIN
    expand = (el == lax.broadcasted_iota(I32, (ne, kw), 0)).astype(BF16)
    sp_exp = (_dot((sp >> 6).astype(F32).astype(BF16), expand) * 64.0
              + _dot((sp & 63).astype(F32).astype(BF16), expand))
    lane1 = lax.broadcasted_iota(I32, (1, kw), 1)
    win_row = jnp.zeros((1, kw), I32)
    for e in range(ne):
        win_row = jnp.where(lane1 // CMB_WIN == e, window(base, e, 0) + 1, win_row)
    oh = (sp_exp - win_row.astype(F32)) == (lane1 % CMB_WIN).astype(F32)
    for e in range(ne):
        first_copy(i, cur, e).wait()
    acc_ref[...] = alpha * x_ref[...] + _dot(oh.astype(BF16), buf_ref[cur])

    lane = lax.broadcasted_iota(I32, (tc, CMB_WIN), 1)
    for e in range(ne):
        off = offs_ref[base + e]
        end = offs_ref[base + ne + e]
        slot = slots[:, e:e + 1]
        n_win = (end - (off // CMB_ALIGN) * CMB_ALIGN + CMB_WIN - 1) // CMB_WIN

        def extra(k, c):
            w0 = window(base, e, k)
            cp = pltpu.make_async_copy(y_hbm.at[bidx, e, pl.ds(w0, CMB_WIN)], buf2_ref, sem2)
            cp.start()
            cp.wait()
            ohk = ((slot - w0) == lane) & ((slot - window(base, e, 0)) >= k * CMB_WIN)
            acc_ref[...] += _dot(ohk.astype(BF16), buf2_ref[...])
            return c

        lax.fori_loop(1, jnp.maximum(n_win, 1), extra, 0)
    o_ref[...] = _layer_norm(acc_ref[...], g_ref[...], b_ref[...])


def _combine(x1, y, slot_tm, offs, g, b, batch, seq_len, alpha):
    n, d = x1.shape
    ne, cap = y.shape[1], y.shape[2]
    tc = min(256, seq_len)
    tps = seq_len // tc
    return pl.pallas_call(
        functools.partial(_combine_kernel, alpha=alpha, cap=cap, tiles_per_seq=tps),
        grid_spec=pltpu.PrefetchScalarGridSpec(
            num_scalar_prefetch=1,
            grid=(n // tc,),
            in_specs=[pl.BlockSpec((tc, d), lambda i, o: (i, 0)),
                      pl.BlockSpec((tc, ne), lambda i, o: (i, 0)),
                      pl.BlockSpec(memory_space=pl.ANY),
                      pl.BlockSpec((1, d), lambda i, o: (0, 0)),
                      pl.BlockSpec((1, d), lambda i, o: (0, 0))],
            out_specs=pl.BlockSpec((tc, d), lambda i, o: (i, 0)),
            scratch_shapes=[pltpu.VMEM((2, ne * CMB_WIN, d), BF16),
                            pltpu.VMEM((CMB_WIN, d), BF16),
                            pltpu.VMEM((tc, d), F32),
                            pltpu.SemaphoreType.DMA((2, ne)),
                            pltpu.SemaphoreType.DMA(())]),
        out_shape=jax.ShapeDtypeStruct((n, d), F32),
        compiler_params=_cparams(("arbitrary",)),
        name="combine_ln",
    )(offs, x1, slot_tm, y, g, b)


def _moe(x1, aff_t, w1, w3, w2, g2, b2, layer, batch, seq_len, alpha):
    cap = EC_FACTOR * seq_len // N_EXPERTS
    idx, gate, slot, offs = _topk(aff_t, cap)
    y = _ffn(x1, idx[..., 0], gate, w1, w3, w2, layer, batch, seq_len)
    tc = min(256, seq_len)
    slot_tm = jnp.transpose(slot.reshape(batch, N_EXPERTS, seq_len), (0, 2, 1)).reshape(batch * seq_len, N_EXPERTS)
    tile_off = offs[:, :, ::tc // LANES, 0]
    tile_off = jnp.concatenate([tile_off, jnp.full((batch, N_EXPERTS, 1), cap, I32)], axis=2)
    tile_off = jnp.transpose(tile_off, (0, 2, 1)).reshape(-1)
    return _combine(x1, y, slot_tm, tile_off, g2, b2, batch, seq_len, alpha)


def kernel(x, positions, w_in, w_out, fno_w, fno_b, lru_conv_w, lru_conv_b, lru_wa, lru_ba, lru_wx, lru_bx,
           lru_lam, dn_conv_w, dn_conv_b, dn_a_log, dn_dt_bias, dn_norm_w, ln1_g, ln1_b, router_w,
           exp_w1, exp_w3, exp_w2, ln2_g, ln2_b):
    batch, seq_len, d = x.shape
    depth = w_in.shape[0]
    gw = GROUP_W
    nh2 = 2 * N_HEADS_G
    alpha = (2.0 * depth) ** 0.25
    n = batch * seq_len

    cos_t, sin_t = _rope_tables(positions)
    tables = _dft_tables(seq_len)

    col = lambda a, b: w_in[:, :, a * gw:b * gw]
    bd = jnp.pad(w_in[:, :, 10 * gw:], ((0, 0), (0, 0), (0, LANES - 2 * nh2)))
    wm = jnp.concatenate([col(0, 1), col(2, 3), col(3, 6), col(9, 10), bd], axis=2).astype(BF16)
    wc = jnp.concatenate([col(1, 2), col(6, 9)], axis=2).astype(BF16)
    cw = jnp.concatenate([lru_conv_w, dn_conv_w], axis=2)
    cb = jnp.concatenate([lru_conv_b, dn_conv_b], axis=1)[:, None, :]
    pad16 = lambda v: jnp.pad(v.reshape(depth, 1, nh2), ((0, 0), (0, 0), (nh2, LANES - 2 * nh2)))
    nalog = pad16(-jnp.exp(dn_a_log.astype(F32)))
    dtb = pad16(dn_dt_bias)
    wf_bd = jax.vmap(_block_diag)(fno_w).astype(BF16)
    bf = fno_b.reshape(depth, 1, gw)
    lru_w = jnp.concatenate([jax.vmap(jax.vmap(_block_diag))(lru_wa),
                             jax.vmap(jax.vmap(_block_diag))(lru_wx)], axis=3).astype(BF16)
    lru_b = jnp.concatenate([lru_ba, lru_bx], axis=2)[:, :, None, :]
    lam = lru_lam[:, :, None, :]
    wo = w_out.astype(BF16)
    nw = jnp.tile(dn_norm_w, (1, N_HEADS_G))[:, None, :]
    rwt = jnp.transpose(router_w, (0, 2, 1)).astype(BF16)

    x2d = x.reshape(n, d)
    for l in range(depth):
        outs = _inproj(x2d, cos_t, sin_t, wm[l], wc[l], cw[l], cb[l], nalog[l], dtb[l], seq_len)
        fa, xc, gg = outs[:3]
        n_att = 3 * len(DIL_PAIRS)
        att_in = outs[3:3 + n_att]
        dq, dk, dv, dsg, dbg = outs[3 + n_att:]
        fy = _fourier(fa, tables, wf_bd[l], bf[l], batch, seq_len)
        yb = _lru(xc, gg, lru_w[l], lru_b[l], lam[l], batch, seq_len)
        att = []
        for p, (window, dil) in enumerate(DIL_PAIRS):
            assert (window // 2) // dil == ATT_R
            att += list(_attn_pattern(*att_in[3 * p:3 * p + 3], dil, batch, seq_len))
        dof, dob = _deltanet(dq, dk, dv, dbg, batch, seq_len)
        x1, aff_t = _outproj(x2d, [fy, yb] + att + [dof, dob, dsg], wo[l], nw[l],
                             ln1_g[l][None, :], ln1_b[l][None, :], rwt[l], batch, seq_len, alpha)
        x2d = _moe(x1, aff_t, exp_w1, exp_w3, exp_w2, ln2_g[l][None, :], ln2_b[l][None, :],
                   l, batch, seq_len, alpha)
    return x2d.reshape(batch, seq_len, d)
```

```python
import functools
import math

import jax
import jax.numpy as jnp
from jax import lax
from jax.experimental import pallas as pl
from jax.experimental.pallas import tpu as pltpu

F32 = jnp.float32
BF16 = jnp.bfloat16
I32 = jnp.int32

GROUP_W = 256
HEAD_DIM = 64
N_HEADS_G = 4
LRU_C = 8.0
DIL_PAIRS = ((128, 1), (512, 4), (2048, 16))
ROPE_THETA = 10000.0
DN_CHUNK = 64
N_EXPERTS = 16
EC_FACTOR = 2
LN_EPS = 1e-5
RMS_EPS = 1e-6
NEG = -1e30

LANES = 128
SUBLANES = 8
VMEM_LIMIT = 56 * 1024 * 1024


def _cparams(sem):
    return pltpu.CompilerParams(dimension_semantics=sem, vmem_limit_bytes=VMEM_LIMIT)


def _dot(a, b):
    return jnp.dot(a, b, preferred_element_type=F32)


def _dot_nt(a, b):
    return lax.dot_general(a, b, (((1,), (1,)), ((), ())), preferred_element_type=F32)


def _bmm(a, b):
    return jnp.einsum('bij,bjk->bik', a, b, preferred_element_type=F32)


def _bmm_nt(a, b):
    return jnp.einsum('bij,bkj->bik', a, b, preferred_element_type=F32)


def _head_sum(t):
    lane = lax.broadcasted_iota(I32, t.shape, 1)
    head = lane // HEAD_DIM
    out = jnp.zeros_like(t)
    for h in range(N_HEADS_G):
        m = head == h
        s = jnp.sum(jnp.where(m, t, 0.0), axis=-1, keepdims=True)
        out = jnp.where(m, s, out)
    return out


def _layer_norm(v, g, b):
    mu = jnp.mean(v, axis=-1, keepdims=True)
    c = v - mu
    var = jnp.mean(c * c, axis=-1, keepdims=True)
    return c * lax.rsqrt(var + LN_EPS) * g + b


def _rope_table_kernel(pos_ref, inv_ref, sgn_ref, cos_ref, sin_ref):
    ang = pos_ref[...] * inv_ref[...]
    cos_ref[...] = jnp.cos(ang)
    sin_ref[...] = jnp.sin(ang) * sgn_ref[...]


def _rope_tables(positions):
    n = positions.size
    half = HEAD_DIM // 2
    inv = ROPE_THETA ** (-jnp.arange(half, dtype=F32) / half)
    inv = jnp.tile(inv, LANES // half)[None, :]
    lane = jnp.arange(LANES)
    sgn = jnp.where((lane % HEAD_DIM) < half, -1.0, 1.0).astype(F32)[None, :]
    pos = positions.reshape(n, 1).astype(F32)
    tm = min(2048, n)
    return pl.pallas_call(
        _rope_table_kernel,
        grid=(n // tm,),
        in_specs=[pl.BlockSpec((tm, 1), lambda i: (i, 0)),
                  pl.BlockSpec((1, LANES), lambda i: (0, 0)),
                  pl.BlockSpec((1, LANES), lambda i: (0, 0))],
        out_specs=[pl.BlockSpec((tm, LANES), lambda i: (i, 0))] * 2,
        out_shape=[jax.ShapeDtypeStruct((n, LANES), F32)] * 2,
        compiler_params=_cparams(("arbitrary",)),
        name="rope_tables",
    )(pos, inv, sgn)


def _inproj_kernel(x_ref, xp_ref, xn_ref, wm_ref, wc_ref, cw_ref, cb_ref, cos_ref, sin_ref,
                   nalog_ref, dtb_ref,
                   fa_ref, xc_ref, gg_ref, *rest, tiles_per_seq):
    n_att = 3 * len(DIL_PAIRS)
    att_refs = rest[:n_att]
    dq_ref, dk_ref, dv_ref, dsg_ref, dbg_ref, scr_ref = rest[n_att:]
    i = pl.program_id(0)
    tm = x_ref.shape[0]
    gw = GROUP_W
    first = (i % tiles_per_seq) == 0
    last = (i % tiles_per_seq) == tiles_per_seq - 1
    xb = x_ref[...].astype(BF16)
    xpb = xp_ref[...].astype(BF16)
    xnb = xn_ref[...].astype(BF16)
    keep_p = jnp.where(first, 0.0, 1.0)
    keep_n = jnp.where(last, 0.0, 1.0)

    def proj(g):
        return _dot(xb, wm_ref[:, g * gw:(g + 1) * gw])

    def conv_group(g):
        w = wc_ref[:, g * gw:(g + 1) * gw]
        ext = jnp.concatenate([_dot(xpb, w) * keep_p, _dot(xb, w), _dot(xnb, w) * keep_n], axis=0)
        out = cb_ref[:, g * gw:(g + 1) * gw]
        for j in range(4):
            out = out + cw_ref[j:j + 1, g * gw:(g + 1) * gw] * ext[6 + j:6 + j + tm, :]
        return out

    fa_ref[...] = proj(0)
    xc_ref[...] = conv_group(0)
    gg_ref[...] = jax.nn.gelu(proj(1))

    cos2 = jnp.concatenate([cos_ref[...], cos_ref[...]], axis=1)
    sin2 = jnp.concatenate([sin_ref[...], sin_ref[...]], axis=1)
    lane = lax.broadcasted_iota(I32, (tm, gw), 1)
    lo = (lane % HEAD_DIM) < (HEAD_DIM // 2)

    def rope(t):
        rot = jnp.where(lo, pltpu.roll(t, gw - HEAD_DIM // 2, 1), pltpu.roll(t, HEAD_DIM // 2, 1))
        return t * cos2 + rot * sin2

    qkv_att = (rope(proj(2)) * (HEAD_DIM ** -0.5), rope(proj(3)), proj(4))
    nl = gw // LANES
    for j, val in enumerate(qkv_att):
        for c in range(nl):
            scr_ref[j, c] = val[:, c * LANES:(c + 1) * LANES]
    for p, (_, dil) in enumerate(DIL_PAIRS):
        for j in range(3):
            ref = att_refs[3 * p + j]
            if dil == 1:
                ref[...] = qkv_att[j].astype(BF16)
            else:
                for r in range(dil):
                    for c in range(nl):
                        c0 = r * gw + c * LANES
                        ref[:, c0:c0 + LANES] = scr_ref[j, c, pl.ds(r, tm // dil, stride=dil), :].astype(BF16)

    def l2n(t):
        return t * lax.rsqrt(_head_sum(t * t) + RMS_EPS)

    dq_ref[...] = l2n(jax.nn.silu(conv_group(1)))
    dk_ref[...] = l2n(jax.nn.silu(conv_group(2)))
    dv_ref[...] = jax.nn.silu(conv_group(3))
    dsg_ref[...] = jax.nn.silu(proj(5))

    bd = _dot(xb, wm_ref[:, 6 * gw:6 * gw + LANES])
    l128 = lax.broadcasted_iota(I32, (tm, LANES), 1)
    beta = jax.nn.sigmoid(bd)
    g = nalog_ref[...] * jax.nn.softplus(bd + dtb_ref[...])
    nh2 = 2 * N_HEADS_G
    dbg_ref[...] = jnp.where(l128 < nh2, beta, jnp.where(l128 < 2 * nh2, g, 0.0))


def _inproj(x2d, cos_t, sin_t, wm, wc, cw, cb, nalog, dtb, seq_len):
    n, d = x2d.shape
    tm = min(512, seq_len)
    nt = n // tm
    tps = seq_len // tm
    hb = tm // SUBLANES
    gw = GROUP_W
    full = lambda shape: pl.BlockSpec(shape, lambda i: (0,) * len(shape))
    row = lambda w: pl.BlockSpec((tm, w), lambda i: (i, 0))
    att_specs, att_shapes = [], []
    for _, dil in DIL_PAIRS:
        att_specs += [pl.BlockSpec((tm // dil, dil * gw), lambda i: (i, 0))] * 3
        att_shapes += [jax.ShapeDtypeStruct((n // dil, dil * gw), BF16)] * 3
    f32_out = lambda w: jax.ShapeDtypeStruct((n, w), F32)
    out_specs = [row(gw)] * 3 + att_specs + [row(gw)] * 4 + [row(LANES)]
    out_shape = [f32_out(gw)] * 3 + att_shapes + [f32_out(gw)] * 4 + [f32_out(LANES)]
    return pl.pallas_call(
        functools.partial(_inproj_kernel, tiles_per_seq=tps),
        grid=(nt,),
        in_specs=[row(d),
                  pl.BlockSpec((SUBLANES, d), lambda i: (jnp.maximum(i * hb - 1, 0), 0)),
                  pl.BlockSpec((SUBLANES, d), lambda i: (jnp.minimum((i + 1) * hb, nt * hb - 1), 0)),
                  full(wm.shape), full(wc.shape), full(cw.shape), full(cb.shape),
                  row(LANES), row(LANES), full(nalog.shape), full(dtb.shape)],
        out_specs=out_specs,
        out_shape=out_shape,
        scratch_shapes=[pltpu.VMEM((3, gw // LANES, tm, LANES), F32)],
        compiler_params=_cparams(("arbitrary",)),
        name="inproj",
    )(x2d, x2d, x2d, wm, wc, cw, cb, cos_t, sin_t, nalog, dtb)


def _fourier1_kernel(x_ref, t_ref, o_ref):
    for j in range(SUBLANES):
        xj = x_ref[:, j, :].astype(BF16)
        o_ref[j] = _dot(t_ref[j], xj)


def _fourier2_kernel(re_ref, im_ref, w2_ref, cc_ref, sc_ref, wf_ref, bf_ref, o_ref, *, scale):
    n2 = re_ref.shape[0]
    for j in range(SUBLANES):
        a = jnp.concatenate([re_ref[:, j, :], im_ref[:, j, :]], axis=0).astype(BF16)
        z = _dot(w2_ref[...], a)
        zr = z[0:n2].astype(BF16)
        zi = z[n2:2 * n2].astype(BF16)
        f = (_dot(zr, cc_ref[...]) + _dot(zi, sc_ref[...])) * scale
        o_ref[:, j, :] = _dot(f.astype(BF16), wf_ref[...]) + bf_ref[...]


def _dft_tables(seq_len):
    n1 = 1 << (int(math.log2(seq_len)) // 2)
    n2 = seq_len // n1
    k1 = jnp.arange(n1, dtype=I32)
    s1 = jnp.arange(n1, dtype=I32)
    s2 = jnp.arange(n2, dtype=I32)
    m = (k1[None, :, None] * (s1[None, None, :] * n2 + s2[:, None, None])) % seq_len
    ang = m.astype(F32) * (2.0 * math.pi / seq_len)
    t1 = jnp.concatenate([jnp.cos(ang), -jnp.sin(ang)], axis=1).astype(BF16)
    k2 = jnp.arange(n2, dtype=I32)
    m2 = (k2[:, None] * s2[None, :]) % n2
    a2 = m2.astype(F32) * (2.0 * math.pi / n2)
    c2, sn2 = jnp.cos(a2), jnp.sin(a2)
    w2 = jnp.concatenate([jnp.concatenate([c2, sn2], axis=1),
                          jnp.concatenate([-sn2, c2], axis=1)], axis=0).astype(BF16)
    c = jnp.arange(HEAD_DIM, dtype=I32)
    ac = ((c[:, None] * c[None, :]) % HEAD_DIM).astype(F32) * (2.0 * math.pi / HEAD_DIM)
    eye = jnp.eye(N_HEADS_G, dtype=F32)
    cc = jnp.kron(eye, jnp.cos(ac)).astype(BF16)
    sc = jnp.kron(eye, jnp.sin(ac)).astype(BF16)
    return n1, n2, t1, w2, cc, sc


def _block_diag(w):
    h, a, b = w.shape
    eye = jnp.eye(h, dtype=w.dtype)
    return (eye[:, None, :, None] * w[:, :, None, :]).reshape(h * a, h * b)


def _fourier(fa, tables, wf_bd, bf, batch, seq_len):
    n1, n2, t1, w2, cc, sc = tables
    gw = GROUP_W
    x4 = fa.reshape(batch, n1, n2, gw)
    a = pl.pallas_call(
        _fourier1_kernel,
        grid=(batch, n2 // SUBLANES),
        in_specs=[pl.BlockSpec((None, n1, SUBLANES, gw), lambda b, i: (b, 0, i, 0)),
                  pl.BlockSpec((SUBLANES, 2 * n1, n1), lambda b, i: (i, 0, 0))],
        out_specs=pl.BlockSpec((None, SUBLANES, 2 * n1, gw), lambda b, i: (b, i, 0, 0)),
        out_shape=jax.ShapeDtypeStruct((batch, n2, 2 * n1, gw), F32),
        compiler_params=_cparams(("arbitrary", "arbitrary")),
        name="fourier_stage1",
    )(x4, t1)
    nb = n1 // SUBLANES
    full = lambda arr: pl.BlockSpec(arr.shape, lambda b, i: (0,) * arr.ndim)
    y = pl.pallas_call(
        functools.partial(_fourier2_kernel, scale=1.0 / math.sqrt(seq_len * HEAD_DIM)),
        grid=(batch, nb),
        in_specs=[pl.BlockSpec((None, n2, SUBLANES, gw), lambda b, i: (b, 0, i, 0)),
                  pl.BlockSpec((None, n2, SUBLANES, gw), lambda b, i: (b, 0, nb + i, 0)),
                  full(w2), full(cc), full(sc), full(wf_bd), full(bf)],
        out_specs=pl.BlockSpec((None, n2, SUBLANES, gw), lambda b, i: (b, 0, i, 0)),
        out_shape=jax.ShapeDtypeStruct((batch, n2, n1, gw), F32),
        compiler_params=_cparams(("arbitrary", "arbitrary")),
        name="fourier_stage2",
    )(a, a, w2, cc, sc, wf_bd, bf)
    return y.reshape(batch * seq_len, gw)


def _lru_kernel(*refs, rev):
    if rev:
        xc_ref, w_ref, b_ref, lam_ref, hf_ref, gg_ref, o_ref, carry_ref = refs
    else:
        xc_ref, w_ref, b_ref, lam_ref, o_ref, carry_ref = refs
    gw = GROUP_W
    t = xc_ref.shape[0]

    @pl.when(pl.program_id(1) == 0)
    def _():
        carry_ref[...] = jnp.zeros_like(carry_ref)

    xc = xc_ref[...]
    gates = _dot(xc.astype(BF16), w_ref[...]) + b_ref[...]
    r = jax.nn.sigmoid(gates[:, 0:gw])
    ig = jax.nn.sigmoid(gates[:, gw:2 * gw])
    log_a = -LRU_C * r * jax.nn.softplus(-lam_ref[...])
    a = jnp.exp(log_a)
    th = jnp.tanh(log_a)
    u = jnp.sqrt(-2.0 * th / (1.0 - th)) * (ig * xc)

    r8 = lax.broadcasted_iota(I32, (t, gw), 0) % SUBLANES
    for s in (1, 2, 4):
        if rev:
            a_s, u_s, m = pltpu.roll(a, t - s, 0), pltpu.roll(u, t - s, 0), r8 < SUBLANES - s
        else:
            a_s, u_s, m = pltpu.roll(a, s, 0), pltpu.roll(u, s, 0), r8 >= s
        u = jnp.where(m, a * u_s + u, u)
        a = jnp.where(m, a * a_s, a)
    carry = carry_ref[...]
    ng = t // SUBLANES
    blocks = [None] * ng
    for g in (range(ng - 1, -1, -1) if rev else range(ng)):
        sl = slice(g * SUBLANES, (g + 1) * SUBLANES)
        blk = u[sl] + a[sl] * carry
        carry = blk[0:1] if rev else blk[SUBLANES - 1:SUBLANES]
        blocks[g] = blk
    carry_ref[...] = carry
    h = jnp.concatenate(blocks, axis=0)
    if rev:
        o_ref[...] = ((hf_ref[...] + h) * gg_ref[...]).astype(o_ref.dtype)
    else:
        o_ref[...] = h


def _lru(xc, gg, w_gate, b_gate, lam, batch, seq_len):
    n, gw = xc.shape
    t = min(512, seq_len)
    nt = seq_len // t
    fwd = lambda b, i: (b * nt + i, 0)
    bwd = lambda b, i: (b * nt + (nt - 1 - i), 0)
    outs = None
    for d, imap in ((0, fwd), (1, bwd)):
        par = lambda arr: pl.BlockSpec((None,) + arr.shape[1:], lambda b, i, d=d: (d,) + (0,) * (arr.ndim - 1))
        tile = pl.BlockSpec((t, gw), imap)
        ins = [xc, w_gate, b_gate, lam]
        specs = [tile, par(w_gate), par(b_gate), par(lam)]
        if d == 1:
            ins += [outs, gg]
            specs += [tile, tile]
        outs = pl.pallas_call(
            functools.partial(_lru_kernel, rev=bool(d)),
            grid=(batch, nt),
            in_specs=specs,
            out_specs=tile,
            out_shape=jax.ShapeDtypeStruct((n, gw), BF16 if d else F32),
            scratch_shapes=[pltpu.VMEM((1, gw), F32)],
            compiler_params=_cparams(("arbitrary", "arbitrary")),
            name="lru_bwd" if d else "lru_fwd",
        )(*ins)
    return outs


ATT_R = 64
ATT_QB = 128


def _attn_kernel(q_ref, kp_ref, k_ref, kn_ref, vp_ref, v_ref, vn_ref, o_ref, l_ref, *, sub_len):
    tq = q_ref.shape[0]
    base = pl.program_id(2) * tq
    kext = jnp.concatenate([kp_ref[...], k_ref[...], kn_ref[...]], axis=0)
    vext = jnp.concatenate([vp_ref[...], v_ref[...], vn_ref[...]], axis=0)
    qb = min(ATT_QB, tq)
    kw = qb + 2 * ATT_R
    nh = N_HEADS_G
    nq = tq // qb
    order = [(jq, h) for jq in range(nq) for h in range(nh)]
    hs = lambda h: slice(h * HEAD_DIM, (h + 1) * HEAD_DIM)
    q3 = jnp.stack([q_ref[jq * qb:(jq + 1) * qb, hs(h)] for jq, h in order])
    k3 = jnp.stack([kext[jq * qb:jq * qb + kw, hs(h)] for jq, h in order])
    v3 = jnp.stack([vext[jq * qb:jq * qb + kw, hs(h)] for jq, h in order])
    shape = (nq * nh, qb, kw)
    qi = lax.broadcasted_iota(I32, shape, 1)
    kj = lax.broadcasted_iota(I32, shape, 2)
    jqi = lax.broadcasted_iota(I32, shape, 0) // nh
    kpos = base + jqi * qb + kj - ATT_R
    mask = (jnp.abs(kj - ATT_R - qi) <= ATT_R) & (kpos >= 0) & (kpos < sub_len)
    s = jnp.where(mask, _bmm_nt(q3, k3), NEG)
    m = jnp.max(s, axis=-1, keepdims=True)
    p = jnp.exp(s - m)
    l = jnp.sum(p, axis=-1, keepdims=True)
    o3 = _bmm(p.astype(BF16), v3) / l
    lse3 = jnp.broadcast_to(m + jnp.log(l), (nq * nh, qb, HEAD_DIM))
    for jq in range(nq):
        o_ref[jq * qb:(jq + 1) * qb, :] = jnp.concatenate([o3[jq * nh + h] for h in range(nh)], axis=1)
        l_ref[jq * qb:(jq + 1) * qb, :] = jnp.concatenate([lse3[jq * nh + h] for h in range(nh)], axis=1)


def _attn_pattern(aq, ak, av, dil, batch, seq_len):
    gw = GROUP_W
    sub = seq_len // dil
    q3, k3, v3 = (t.reshape(batch, sub, dil * gw) for t in (aq, ak, av))
    tq = min(1024, sub)
    nq = sub // tq
    hb = tq // ATT_R
    nhb = sub // ATT_R
    main = pl.BlockSpec((None, tq, gw), lambda b, r, i: (b, i, r))
    prev = pl.BlockSpec((None, ATT_R, gw), lambda b, r, i: (b, jnp.maximum(i * hb - 1, 0), r))
    nxt = pl.BlockSpec((None, ATT_R, gw), lambda b, r, i: (b, jnp.minimum((i + 1) * hb, nhb - 1), r))
    o, l = pl.pallas_call(
        functools.partial(_attn_kernel, sub_len=sub),
        grid=(batch, dil, nq),
        in_specs=[main, prev, main, nxt, prev, main, nxt],
        out_specs=[main, main],
        out_shape=[jax.ShapeDtypeStruct((batch, sub, dil * gw), F32)] * 2,
        compiler_params=_cparams(("arbitrary", "arbitrary", "arbitrary")),
        name=f"dilated_attn_d{dil}",
    )(q3, k3, k3, k3, v3, v3, v3)
    return o.reshape(batch * sub, dil * gw), l.reshape(batch * sub, dil * gw)


DN_TILE = 256


def _deltanet_kernel(qf_ref, kf_ref, vf_ref, gf_ref, qb_ref, kb_ref, vb_ref, gb_ref,
                     of_ref, ob_ref, state_ref):
    nbat, t = qf_ref.shape[0], qf_ref.shape[1]
    cl = DN_CHUNK
    nc = t // cl
    nh = N_HEADS_G
    hd = HEAD_DIM
    gw = GROUP_W
    nb = nbat * 2 * nc

    @pl.when(pl.program_id(0) == 0)
    def _():
        state_ref[...] = jnp.zeros_like(state_ref)

    rc = lax.broadcasted_iota(I32, (t, LANES), 0) % cl
    gcs, gts = [], []
    for d, g_ref in ((0, gf_ref), (1, gb_ref)):
        gcl, gtl = [], []
        for bi in range(nbat):
            gc = g_ref[bi]
            s = 1
            while s < cl:
                if d:
                    gc = jnp.where(rc < cl - s, gc + pltpu.roll(gc, t - s, 0), gc)
                else:
                    gc = jnp.where(rc >= s, gc + pltpu.roll(gc, s, 0), gc)
                s *= 2
            gcl.append(gc)
            gtl.append(gc.T)
        gcs.append(gcl)
        gts.append(gtl)

    order = [(bi, d, c) for bi in range(nbat) for d in range(2) for c in range(nc)]
    data = ((qf_ref, kf_ref, vf_ref, gf_ref), (qb_ref, kb_ref, vb_ref, gb_ref))
    rows = lambda c: slice(c * cl, (c + 1) * cl)

    q3 = jnp.stack([data[d][0][bi, rows(c), :] for bi, d, c in order])
    k3 = jnp.stack([data[d][1][bi, rows(c), :] for bi, d, c in order])
    v3 = jnp.stack([data[d][2][bi, rows(c), :] for bi, d, c in order])

    def lane_expand(src_of, col_of):
        blocks = []
        for bi, d, c in order:
            src = src_of(d, bi)[rows(c), :]
            blocks.append(jnp.concatenate(
                [jnp.broadcast_to(src[:, col_of(d, h):col_of(d, h) + 1], (cl, hd)) for h in range(nh)], axis=1))
        return jnp.stack(blocks)

    g_col = lambda d, h: 2 * nh + d * nh + h
    beta3 = lane_expand(lambda d, bi: data[d][3][bi], lambda d, h: d * nh + h)
    gcc3 = lane_expand(lambda d, bi: gcs[d][bi], g_col)
    gcr3 = jnp.stack([jnp.concatenate([gts[d][bi][g_col(d, h):g_col(d, h) + 1, rows(c)] for h in range(nh)], axis=1)
                      for bi, d, c in order])

    ri = lax.broadcasted_iota(I32, (nb, cl, gw), 1)
    ci = lax.broadcasted_iota(I32, (nb, cl, gw), 2) % hd
    isb = (lax.broadcasted_iota(I32, (nb, cl, gw), 0) // nc) % 2 == 1
    r2 = jnp.where(isb, ci, ri)
    c2 = jnp.where(isb, ri, ci)
    incl = r2 >= c2
    strict = r2 > c2
    eye = (ri == ci).astype(F32)

    def block_diag(x):
        n = x.shape[0]
        same_head = (lax.broadcasted_iota(I32, (n, gw, gw), 1) // hd) == (lax.broadcasted_iota(I32, (n, gw, gw), 2) // hd)
        return jnp.where(same_head, jnp.concatenate([x.astype(BF16)] * nh, axis=1), jnp.zeros((), BF16))

    decay = jnp.where(incl, jnp.exp(jnp.where(incl, gcc3 - gcr3, 0.0)), 0.0)
    kb3 = k3 * beta3
    vb3 = v3 * beta3
    qs = q3 * (hd ** -0.5)
    mk = _bmm_nt(jnp.concatenate([kb3, qs], axis=1).astype(BF16), block_diag(k3))
    m = mk[:, 0:cl, :] * jnp.where(strict, decay, 0.0)
    a3 = (mk[:, cl:2 * cl, :] * decay).astype(BF16)
    blk = lambda s: (ri // s) == (ci // s)
    x4 = jnp.where(blk(4), -m, 0.0)
    x4sq = _bmm(x4.astype(BF16), block_diag(x4))
    tm = eye + x4
    tm = tm + _bmm(tm.astype(BF16), block_diag(x4sq))
    for s in (4, 8, 16, 32):
        cm = jnp.where(blk(2 * s) & jnp.logical_not(blk(s)), m, 0.0)
        tc_ = _bmm(cm.astype(BF16), block_diag(tm))
        tm = tm - _bmm(tm.astype(BF16), block_diag(tc_))
    eg = jnp.exp(gcc3)
    tmb = tm.astype(BF16)
    w3 = _bmm(tmb, block_diag(vb3))
    u3 = _bmm(tmb, block_diag(kb3 * eg)).astype(BF16)
    qd3 = (qs * eg).astype(BF16)
    isb1 = (lax.broadcasted_iota(I32, (nb, 1, gw), 0) // nc) % 2 == 1
    gl3 = jnp.where(isb1, gcc3[:, 0:1, :], gcc3[:, cl - 1:cl, :])
    kd3 = k3 * jnp.exp(gl3 - gcc3)
    kdt3 = _bmm_nt(eye.astype(BF16), block_diag(kd3)).astype(BF16)
    egl3 = jnp.exp(gl3)
    uq3 = jnp.concatenate([u3, qd3], axis=1)
    ak3 = jnp.concatenate([a3, kdt3], axis=1)

    def step_bodies(arr, c):
        parts = []
        for bi in range(nbat):
            f0 = bi * 2 * nc + c
            b0 = bi * 2 * nc + nc + (nc - 1 - c)
            parts += [arr[f0:f0 + 1], arr[b0:b0 + 1]]
        return jnp.concatenate(parts, axis=0)

    st = state_ref[...]
    for c in range(nc):
        us = _bmm(step_bodies(uq3, c), block_diag(st))
        v_new = step_bodies(w3, c) - us[:, 0:cl, :]
        av = _bmm(step_bodies(ak3, c), block_diag(v_new))
        o2 = us[:, cl:2 * cl, :] + av[:, 0:cl, :]
        st = st * step_bodies(egl3, c) + av[:, cl:2 * cl, :]
        for bi in range(nbat):
            of_ref[bi, rows(c), :] = o2[2 * bi]
            ob_ref[bi, rows(nc - 1 - c), :] = o2[2 * bi + 1]
    state_ref[...] = st


def _deltanet(dq, dk, dv, dbg, batch, seq_len):
    n, gw = dq.shape
    t = min(DN_TILE, seq_len)
    nt = seq_len // t
    fwd = lambda i: (0, i, 0)
    bwd = lambda i: (0, nt - 1 - i, 0)
    tf, tb = pl.BlockSpec((batch, t, gw), fwd), pl.BlockSpec((batch, t, gw), bwd)
    gf, gb = pl.BlockSpec((batch, t, LANES), fwd), pl.BlockSpec((batch, t, LANES), bwd)
    dq, dk, dv = (a.reshape(batch, seq_len, gw) for a in (dq, dk, dv))
    dbg = dbg.reshape(batch, seq_len, LANES)
    outs = pl.pallas_call(
        _deltanet_kernel,
        grid=(nt,),
        in_specs=[tf, tf, tf, gf, tb, tb, tb, gb],
        out_specs=[tf, tb],
        out_shape=[jax.ShapeDtypeStruct((batch, seq_len, gw), F32)] * 2,
        scratch_shapes=[pltpu.VMEM((2 * batch, HEAD_DIM, GROUP_W), F32)],
        compiler_params=_cparams(("arbitrary",)),
        name="deltanet",
    )(dq, dk, dv, dbg, dq, dk, dv, dbg)
    return [o.reshape(n, gw) for o in outs]


def _outproj_kernel(x_ref, fy_ref, yb_ref, o1_ref, l1_ref, o2_ref, l2_ref, o3_ref, l3_ref,
                    dof_ref, dob_ref, dsg_ref, wo_ref, nw_ref, g_ref, b_ref, rwt_ref,
                    x1_ref, aff_ref, scr_ref, *, alpha):
    gw = GROUP_W
    tm = x_ref.shape[0]
    att = []
    for p, ((_, dil), refs) in enumerate(zip(DIL_PAIRS, ((o1_ref, l1_ref), (o2_ref, l2_ref), (o3_ref, l3_ref)))):
        for j, ref in enumerate(refs):
            if dil == 1:
                att.append(ref[...])
            else:
                nl = gw // LANES
                for r in range(dil):
                    for c in range(nl):
                        c0 = r * gw + c * LANES
                        scr_ref[2 * p + j, c, pl.ds(r, tm // dil, stride=dil), :] = ref[:, c0:c0 + LANES]
                att.append(jnp.concatenate([scr_ref[2 * p + j, c] for c in range(nl)], axis=1))
    o1, l1, o2, l2, o3, l3 = att
    m = jnp.maximum(jnp.maximum(l1, l2), l3)
    w1, w2, w3 = jnp.exp(l1 - m), jnp.exp(l2 - m), jnp.exp(l3 - m)
    yc = (w1 * o1 + w2 * o2 + w3 * o3) / (w1 + w2 + w3)
    od = dof_ref[...] + dob_ref[...]
    ms = _head_sum(od * od) * (1.0 / HEAD_DIM)
    yd = od * lax.rsqrt(ms + RMS_EPS) * nw_ref[...] * dsg_ref[...]
    y = (_dot(fy_ref[...].astype(BF16), wo_ref[0:gw, :])
         + _dot(yb_ref[...].astype(BF16), wo_ref[gw:2 * gw, :])
         + _dot(yc.astype(BF16), wo_ref[2 * gw:3 * gw, :])
         + _dot(yd.astype(BF16), wo_ref[3 * gw:4 * gw, :]))
    x1 = _layer_norm(alpha * x_ref[...] + y, g_ref[...], b_ref[...])
    x1_ref[...] = x1
    logits = _dot_nt(rwt_ref[...], x1.astype(BF16))
    mx = jnp.max(logits, axis=0, keepdims=True)
    e = jnp.exp(logits - mx)
    aff_ref[...] = e / jnp.sum(e, axis=0, keepdims=True)


def _outproj(x2d, parts, wo, nw, g, b, rwt, batch, seq_len, alpha):
    n, d = x2d.shape
    gw = GROUP_W
    tm = min(512, seq_len)
    tps = seq_len // tm
    row = lambda w: pl.BlockSpec((tm, w), lambda i: (i, 0))
    full = lambda arr: pl.BlockSpec(arr.shape, lambda i: (0,) * arr.ndim)
    att_specs = []
    for _, dil in DIL_PAIRS:
        att_specs += [pl.BlockSpec((tm // dil, dil * gw), lambda i: (i, 0))] * 2
    return pl.pallas_call(
        functools.partial(_outproj_kernel, alpha=alpha),
        grid=(n // tm,),
        in_specs=[row(d)] + [row(gw)] * 2 + att_specs + [row(gw)] * 3
                 + [full(wo), full(nw), full(g), full(b), full(rwt)],
        out_specs=[row(d), pl.BlockSpec((None, N_EXPERTS, tm), lambda i: (i // tps, 0, i % tps))],
        out_shape=[jax.ShapeDtypeStruct((n, d), F32),
                   jax.ShapeDtypeStruct((batch, N_EXPERTS, seq_len), F32)],
        scratch_shapes=[pltpu.VMEM((2 * len(DIL_PAIRS), gw // LANES, tm, LANES), F32)],
        compiler_params=_cparams(("arbitrary",)),
        name="outproj_ln_router",
    )(x2d, *parts, wo, nw, g, b, rwt)


def _topk_kernel(a_ref, idx_ref, gate_ref, slot_ref, off_ref, *, cap):
    ng = a_ref.shape[0]
    v = a_ref[...]
    bits = pltpu.bitcast(v, I32)

    thr = jnp.int32(0)
    for shift in range(28, -1, -4):
        n_cand = 7 if shift == 28 else 15
        digit = jnp.int32(0)
        for c in range(1, n_cand + 1):
            cnt = jnp.sum((bits >= (thr | jnp.int32(c << shift))).astype(I32))
            digit = digit + (cnt >= cap).astype(I32)
        thr = thr | (digit << shift)
    gt = bits > thr
    eq = bits == thr
    need_eq = cap - jnp.sum(gt.astype(I32))

    li = lax.broadcasted_iota(I32, (LANES, LANES), 0)
    lj = lax.broadcasted_iota(I32, (LANES, LANES), 1)
    ut_incl = (li <= lj).astype(BF16)
    gi = lax.broadcasted_iota(I32, (ng, ng), 0)
    gj = lax.broadcasted_iota(I32, (ng, ng), 1)
    lt_strict = (gj < gi).astype(BF16)

    def prefix(mask):
        p1 = _dot(mask.astype(BF16), ut_incl)
        tot = jnp.broadcast_to(p1[:, LANES - 1:LANES], (ng, LANES))
        return p1, _dot(lt_strict, tot.astype(BF16)), tot

    p1e, offe, _ = prefix(eq)
    rank_eq = p1e - eq.astype(F32) + offe
    sel = gt | (eq & (rank_eq < need_eq.astype(F32)))
    p1, offs, tot = prefix(sel)
    slot_ref[...] = jnp.where(sel, (p1 + offs).astype(I32) - 1, -1)
    off_ref[...] = offs.astype(I32)

    j = lax.broadcasted_iota(I32, (cap, 1), 0).astype(F32)
    ends_row = (offs + tot).T[0:1, :]
    offs_row = offs.T[0:1, :]
    gj_ = jnp.sum((ends_row <= j).astype(I32), axis=-1, keepdims=True)
    oh = lax.broadcasted_iota(I32, (cap, ng), 1) == gj_
    off_j = jnp.sum(jnp.where(oh, offs_row, 0.0), axis=-1, keepdims=True)
    ohb = oh.astype(BF16)
    prow = _dot(ohb, p1.astype(BF16))
    lo = jnp.sum((prow <= (j - off_j)).astype(I32), axis=-1, keepdims=True)
    idx_ref[...] = gj_ * LANES + lo
    h1 = v.astype(BF16)
    r1 = v - h1.astype(F32)
    h2 = r1.astype(BF16)
    h3 = (r1 - h2.astype(F32)).astype(BF16)
    arow = _dot(ohb, h1) + _dot(ohb, h2) + _dot(ohb, h3)
    lane = lax.broadcasted_iota(I32, (cap, LANES), 1)
    gate_ref[...] = jnp.sum(jnp.where(lane == lo, arow, 0.0), axis=-1, keepdims=True)


def _topk(aff_t, cap):
    batch, ne, seq_len = aff_t.shape
    ng = seq_len // LANES
    a4 = aff_t.reshape(batch, ne, ng, LANES)
    grp = pl.BlockSpec((None, None, ng, LANES), lambda b, e: (b, e, 0, 0))
    col = pl.BlockSpec((None, None, cap, 1), lambda b, e: (b, e, 0, 0))
    return pl.pallas_call(
        functools.partial(_topk_kernel, cap=cap),
        grid=(batch, ne),
        in_specs=[grp],
        out_specs=[col, col, grp, grp],
        out_shape=[jax.ShapeDtypeStruct((batch, ne, cap, 1), I32),
                   jax.ShapeDtypeStruct((batch, ne, cap, 1), F32),
                   jax.ShapeDtypeStruct((batch, ne, ng, LANES), I32),
                   jax.ShapeDtypeStruct((batch, ne, ng, LANES), I32)],
        compiler_params=_cparams(("arbitrary", "arbitrary")),
        name="expert_topk",
    )(a4)


FFN_ROWS = 512


def _ffn_kernel(idx_ref, idxn_ref, x_hbm, gate_ref, w1_ref, w3_ref, w2_ref, y_hbm,
                xg_ref, gbuf_ref, acc_ref, sem, osem, *, seq_len):
    b = pl.program_id(0)
    e = pl.program_id(1)
    f = pl.program_id(2)
    ne = pl.num_programs(1)
    blk_id = b * ne + e
    n_blocks = pl.num_programs(0) * ne
    cap = xg_ref.shape[0]
    rb = min(FFN_ROWS, cap)

    def row_copy(iref, base, r):
        return pltpu.make_async_copy(x_hbm.at[pl.ds(base + iref[0, 0, r], 1)], gbuf_ref.at[pl.ds(r, 1)], sem)

    def gather_start(iref, base):
        def body(r, c):
            row_copy(iref, base, r).start()
            return c
        lax.fori_loop(0, cap, body, 0, unroll=8)

    @pl.when(f == 0)
    def _():
        @pl.when(blk_id == 0)
        def _():
            gather_start(idx_ref, b * seq_len)

        def wait_body(r, c):
            row_copy(idx_ref, b * seq_len, r).wait()
            return c
        lax.fori_loop(0, cap, wait_body, 0, unroll=8)
        for blk in range(cap // rb):
            rs = slice(blk * rb, (blk + 1) * rb)
            xg_ref[rs, :] = gbuf_ref[rs, :].astype(BF16)
            acc_ref[rs, :] = jnp.zeros((rb, acc_ref.shape[1]), F32)

        @pl.when(blk_id + 1 < n_blocks)
        def _():
            gather_start(idxn_ref, ((blk_id + 1) // ne) * seq_len)

    fc = w1_ref.shape[1]
    w13 = jnp.concatenate([w1_ref[...].astype(BF16), w3_ref[...].astype(BF16)], axis=1)
    w2 = w2_ref[...].astype(BF16)
    for blk in range(cap // rb):
        rs = slice(blk * rb, (blk + 1) * rb)
        h13 = _dot(xg_ref[rs, :], w13)
        h = (jax.nn.silu(h13[:, 0:fc]) * h13[:, fc:2 * fc]).astype(BF16)
        acc_ref[rs, :] += _dot(h, w2)

    @pl.when(f == pl.num_programs(2) - 1)
    def _():
        for blk in range(cap // rb):
            rs = slice(blk * rb, (blk + 1) * rb)
            xg_ref[rs, :] = (acc_ref[rs, :] * gate_ref[rs, :]).astype(BF16)
        out_copy = pltpu.make_async_copy(xg_ref, y_hbm.at[b, e], osem)
        out_copy.start()
        out_copy.wait()


def _ffn(x1, idx, gate, w1, w3, w2, layer, batch, seq_len):
    n, d = x1.shape
    ne, cap = idx.shape[1], idx.shape[2]
    dff = w1.shape[-1]
    fc = 896 if dff % 896 == 0 else dff
    nf = dff // fc
    nblk = batch * ne
    idx3 = idx.reshape(nblk, 1, cap)
    return pl.pallas_call(
        functools.partial(_ffn_kernel, seq_len=seq_len),
        grid=(batch, ne, nf),
        in_specs=[pl.BlockSpec((1, 1, cap), lambda b, e, f: (b * ne + e, 0, 0), memory_space=pltpu.SMEM),
                  pl.BlockSpec((1, 1, cap), lambda b, e, f: (jnp.minimum(b * ne + e + 1, nblk - 1), 0, 0),
                               memory_space=pltpu.SMEM),
                  pl.BlockSpec(memory_space=pl.ANY),
                  pl.BlockSpec((None, None, cap, 1), lambda b, e, f: (b, e, 0, 0)),
                  pl.BlockSpec((None, None, d, fc), lambda b, e, f: (layer, e, 0, f)),
                  pl.BlockSpec((None, None, d, fc), lambda b, e, f: (layer, e, 0, f)),
                  pl.BlockSpec((None, None, fc, d), lambda b, e, f: (layer, e, f, 0))],
        out_specs=pl.BlockSpec(memory_space=pl.ANY),
        out_shape=jax.ShapeDtypeStruct((batch, ne, cap, d), BF16),
        scratch_shapes=[pltpu.VMEM((cap, d), BF16),
                        pltpu.VMEM((cap, d), F32),
                        pltpu.VMEM((cap, d), F32),
                        pltpu.SemaphoreType.DMA(()),
                        pltpu.SemaphoreType.DMA(())],
        compiler_params=_cparams(("arbitrary",) * 3),
        name="expert_ffn",
    )(idx3, idx3, x1, gate, w1, w3, w2)


CMB_WIN = 64
CMB_ALIGN = 16


def _combine_kernel(offs_ref, x_ref, slot_ref, y_hbm, g_ref, b_ref, o_ref, buf_ref, buf2_ref, acc_ref,
                    sem, sem2, *, alpha, cap, tiles_per_seq):
    i = pl.program_id(0)
    nsteps = pl.num_programs(0)
    tc = x_ref.shape[0]
    ne = N_EXPERTS
    kw = ne * CMB_WIN

    def tile(step):
        bidx = step // tiles_per_seq
        return bidx, (bidx * (tiles_per_seq + 1) + step % tiles_per_seq) * ne

    def window(base, e, k):
        start = (offs_ref[base + e] // CMB_ALIGN) * CMB_ALIGN + k * CMB_WIN
        return pl.multiple_of(jnp.minimum(start, cap - CMB_WIN), CMB_ALIGN)

    def first_copy(step, buf, e):
        bidx, base = tile(step)
        return pltpu.make_async_copy(y_hbm.at[bidx, e, pl.ds(window(base, e, 0), CMB_WIN)],
                                     buf_ref.at[buf, pl.ds(e * CMB_WIN, CMB_WIN)], sem.at[buf, e])

    cur = i % 2

    @pl.when(i == 0)
    def _():
        for e in range(ne):
            first_copy(i, cur, e).start()

    @pl.when(i + 1 < nsteps)
    def _():
        for e in range(ne):
            first_copy(i + 1, 1 - cur, e).start()

    bidx, base = tile(i)
    slots = slot_ref[...]
    sp = slots + 1
    el = lax.broadcasted_iota(I32, (ne, kw), 1) // CMB_WIN
    expand = (el == lax.broadcasted_iota(I32, (ne, kw), 0)).astype(BF16)
    sp_exp = (_dot((sp >> 6).astype(F32).astype(BF16), expand) * 64.0
              + _dot((sp & 63).astype(F32).astype(BF16), expand))
    lane1 = lax.broadcasted_iota(I32, (1, kw), 1)
    win_row = jnp.zeros((1, kw), I32)
    for e in range(ne):
        win_row = jnp.where(lane1 // CMB_WIN == e, window(base, e, 0) + 1, win_row)
    oh = (sp_exp - win_row.astype(F32)) == (lane1 % CMB_WIN).astype(F32)
    for e in range(ne):
        first_copy(i, cur, e).wait()
    acc_ref[...] = alpha * x_ref[...] + _dot(oh.astype(BF16), buf_ref[cur])

    lane = lax.broadcasted_iota(I32, (tc, CMB_WIN), 1)
    for e in range(ne):
        off = offs_ref[base + e]
        end = offs_ref[base + ne + e]
        slot = slots[:, e:e + 1]
        n_win = (end - (off // CMB_ALIGN) * CMB_ALIGN + CMB_WIN - 1) // CMB_WIN

        def extra(k, c):
            w0 = window(base, e, k)
            cp = pltpu.make_async_copy(y_hbm.at[bidx, e, pl.ds(w0, CMB_WIN)], buf2_ref, sem2)
            cp.start()
            cp.wait()
            ohk = ((slot - w0) == lane) & ((slot - window(base, e, 0)) >= k * CMB_WIN)
            acc_ref[...] += _dot(ohk.astype(BF16), buf2_ref[...])
            return c

        lax.fori_loop(1, jnp.maximum(n_win, 1), extra, 0)
    o_ref[...] = _layer_norm(acc_ref[...], g_ref[...], b_ref[...])


def _combine(x1, y, slot_tm, offs, g, b, batch, seq_len, alpha):
    n, d = x1.shape
    ne, cap = y.shape[1], y.shape[2]
    tc = min(256, seq_len)
    tps = seq_len // tc
    return pl.pallas_call(
        functools.partial(_combine_kernel, alpha=alpha, cap=cap, tiles_per_seq=tps),
        grid_spec=pltpu.PrefetchScalarGridSpec(
            num_scalar_prefetch=1,
            grid=(n // tc,),
            in_specs=[pl.BlockSpec((tc, d), lambda i, o: (i, 0)),
                      pl.BlockSpec((tc, ne), lambda i, o: (i, 0)),
                      pl.BlockSpec(memory_space=pl.ANY),
                      pl.BlockSpec((1, d), lambda i, o: (0, 0)),
                      pl.BlockSpec((1, d), lambda i, o: (0, 0))],
            out_specs=pl.BlockSpec((tc, d), lambda i, o: (i, 0)),
            scratch_shapes=[pltpu.VMEM((2, ne * CMB_WIN, d), BF16),
                            pltpu.VMEM((CMB_WIN, d), BF16),
                            pltpu.VMEM((tc, d), F32),
                            pltpu.SemaphoreType.DMA((2, ne)),
                            pltpu.SemaphoreType.DMA(())]),
        out_shape=jax.ShapeDtypeStruct((n, d), F32),
        compiler_params=_cparams(("arbitrary",)),
        name="combine_ln",
    )(offs, x1, slot_tm, y, g, b)


def _moe(x1, aff_t, w1, w3, w2, g2, b2, layer, batch, seq_len, alpha):
    cap = EC_FACTOR * seq_len // N_EXPERTS
    idx, gate, slot, offs = _topk(aff_t, cap)
    y = _ffn(x1, idx[..., 0], gate, w1, w3, w2, layer, batch, seq_len)
    tc = min(256, seq_len)
    slot_tm = jnp.transpose(slot.reshape(batch, N_EXPERTS, seq_len), (0, 2, 1)).reshape(batch * seq_len, N_EXPERTS)
    tile_off = offs[:, :, ::tc // LANES, 0]
    tile_off = jnp.concatenate([tile_off, jnp.full((batch, N_EXPERTS, 1), cap, I32)], axis=2)
    tile_off = jnp.transpose(tile_off, (0, 2, 1)).reshape(-1)
    return _combine(x1, y, slot_tm, tile_off, g2, b2, batch, seq_len, alpha)


def kernel(x, positions, w_in, w_out, fno_w, fno_b, lru_conv_w, lru_conv_b, lru_wa, lru_ba, lru_wx, lru_bx,
           lru_lam, dn_conv_w, dn_conv_b, dn_a_log, dn_dt_bias, dn_norm_w, ln1_g, ln1_b, router_w,
           exp_w1, exp_w3, exp_w2, ln2_g, ln2_b):
    batch, seq_len, d = x.shape
    depth = w_in.shape[0]
    gw = GROUP_W
    nh2 = 2 * N_HEADS_G
    alpha = (2.0 * depth) ** 0.25
    n = batch * seq_len

    cos_t, sin_t = _rope_tables(positions)
    tables = _dft_tables(seq_len)

    col = lambda a, b: w_in[:, :, a * gw:b * gw]
    bd = jnp.pad(w_in[:, :, 10 * gw:], ((0, 0), (0, 0), (0, LANES - 2 * nh2)))
    wm = jnp.concatenate([col(0, 1), col(2, 3), col(3, 6), col(9, 10), bd], axis=2).astype(BF16)
    wc = jnp.concatenate([col(1, 2), col(6, 9)], axis=2).astype(BF16)
    cw = jnp.concatenate([lru_conv_w, dn_conv_w], axis=2)
    cb = jnp.concatenate([lru_conv_b, dn_conv_b], axis=1)[:, None, :]
    pad16 = lambda v: jnp.pad(v.reshape(depth, 1, nh2), ((0, 0), (0, 0), (nh2, LANES - 2 * nh2)))
    nalog = pad16(-jnp.exp(dn_a_log.astype(F32)))
    dtb = pad16(dn_dt_bias)
    wf_bd = jax.vmap(_block_diag)(fno_w).astype(BF16)
    bf = fno_b.reshape(depth, 1, gw)
    lru_w = jnp.concatenate([jax.vmap(jax.vmap(_block_diag))(lru_wa),
                             jax.vmap(jax.vmap(_block_diag))(lru_wx)], axis=3).astype(BF16)
    lru_b = jnp.concatenate([lru_ba, lru_bx], axis=2)[:, :, None, :]
    lam = lru_lam[:, :, None, :]
    wo = w_out.astype(BF16)
    nw = jnp.tile(dn_norm_w, (1, N_HEADS_G))[:, None, :]
    rwt = jnp.transpose(router_w, (0, 2, 1)).astype(BF16)

    x2d = x.reshape(n, d)
    for l in range(depth):
        outs = _inproj(x2d, cos_t, sin_t, wm[l], wc[l], cw[l], cb[l], nalog[l], dtb[l], seq_len)
        fa, xc, gg = outs[:3]
        n_att = 3 * len(DIL_PAIRS)
        att_in = outs[3:3 + n_att]
        dq, dk, dv, dsg, dbg = outs[3 + n_att:]
        fy = _fourier(fa, tables, wf_bd[l], bf[l], batch, seq_len)
        yb = _lru(xc, gg, lru_w[l], lru_b[l], lam[l], batch, seq_len)
        att = []
        for p, (window, dil) in enumerate(DIL_PAIRS):
            assert (window // 2) // dil == ATT_R
            att += list(_attn_pattern(*att_in[3 * p:3 * p + 3], dil, batch, seq_len))
        dof, dob = _deltanet(dq, dk, dv, dbg, batch, seq_len)
        x1, aff_t = _outproj(x2d, [fy, yb] + att + [dof, dob, dsg], wo[l], nw[l],
                             ln1_g[l][None, :], ln1_b[l][None, :], rwt[l], batch, seq_len, alpha)
        x2d = _moe(x1, aff_t, exp_w1, exp_w3, exp_w2, ln2_g[l][None, :], ln2_b[l][None, :],
                   l, batch, seq_len, alpha)
    return x2d.reshape(batch, seq_len, d)
```

```python
import functools
import math

import jax
import jax.numpy as jnp
from jax import lax
from jax.experimental import pallas as pl
from jax.experimental.pallas import tpu as pltpu

F32 = jnp.float32
BF16 = jnp.bfloat16
I32 = jnp.int32

GROUP_W = 256
HEAD_DIM = 64
N_HEADS_G = 4
LRU_C = 8.0
DIL_PAIRS = ((128, 1), (512, 4), (2048, 16))
ROPE_THETA = 10000.0
DN_CHUNK = 64
N_EXPERTS = 16
EC_FACTOR = 2
LN_EPS = 1e-5
RMS_EPS = 1e-6
NEG = -1e30

LANES = 128
SUBLANES = 8
VMEM_LIMIT = 56 * 1024 * 1024


def _cparams(sem):
    return pltpu.CompilerParams(dimension_semantics=sem, vmem_limit_bytes=VMEM_LIMIT)


def _dot(a, b):
    return jnp.dot(a, b, preferred_element_type=F32)


def _dot_nt(a, b):
    return lax.dot_general(a, b, (((1,), (1,)), ((), ())), preferred_element_type=F32)


def _bmm(a, b):
    return jnp.einsum('bij,bjk->bik', a, b, preferred_element_type=F32)


def _bmm_nt(a, b):
    return jnp.einsum('bij,bkj->bik', a, b, preferred_element_type=F32)


def _head_sum(t):
    lane = lax.broadcasted_iota(I32, t.shape, 1)
    head = lane // HEAD_DIM
    out = jnp.zeros_like(t)
    for h in range(N_HEADS_G):
        m = head == h
        s = jnp.sum(jnp.where(m, t, 0.0), axis=-1, keepdims=True)
        out = jnp.where(m, s, out)
    return out


def _layer_norm(v, g, b):
    mu = jnp.mean(v, axis=-1, keepdims=True)
    c = v - mu
    var = jnp.mean(c * c, axis=-1, keepdims=True)
    return c * lax.rsqrt(var + LN_EPS) * g + b


def _rope_table_kernel(pos_ref, inv_ref, sgn_ref, cos_ref, sin_ref):
    ang = pos_ref[...] * inv_ref[...]
    cos_ref[...] = jnp.cos(ang)
    sin_ref[...] = jnp.sin(ang) * sgn_ref[...]


def _rope_tables(positions):
    n = positions.size
    half = HEAD_DIM // 2
    inv = ROPE_THETA ** (-jnp.arange(half, dtype=F32) / half)
    inv = jnp.tile(inv, LANES // half)[None, :]
    lane = jnp.arange(LANES)
    sgn = jnp.where((lane % HEAD_DIM) < half, -1.0, 1.0).astype(F32)[None, :]
    pos = positions.reshape(n, 1).astype(F32)
    tm = min(2048, n)
    return pl.pallas_call(
        _rope_table_kernel,
        grid=(n // tm,),
        in_specs=[pl.BlockSpec((tm, 1), lambda i: (i, 0)),
                  pl.BlockSpec((1, LANES), lambda i: (0, 0)),
                  pl.BlockSpec((1, LANES), lambda i: (0, 0))],
        out_specs=[pl.BlockSpec((tm, LANES), lambda i: (i, 0))] * 2,
        out_shape=[jax.ShapeDtypeStruct((n, LANES), F32)] * 2,
        compiler_params=_cparams(("arbitrary",)),
        name="rope_tables",
    )(pos, inv, sgn)


def _inproj_kernel(x_ref, xp_ref, xn_ref, wm_ref, wc_ref, cw_ref, cb_ref, cos_ref, sin_ref,
                   nalog_ref, dtb_ref,
                   fa_ref, xc_ref, gg_ref, *rest, tiles_per_seq):
    n_att = 3 * len(DIL_PAIRS)
    att_refs = rest[:n_att]
    dq_ref, dk_ref, dv_ref, dsg_ref, dbg_ref, scr_ref = rest[n_att:]
    i = pl.program_id(0)
    tm = x_ref.shape[0]
    gw = GROUP_W
    first = (i % tiles_per_seq) == 0
    last = (i % tiles_per_seq) == tiles_per_seq - 1
    xb = x_ref[...].astype(BF16)
    xpb = xp_ref[...].astype(BF16)
    xnb = xn_ref[...].astype(BF16)
    keep_p = jnp.where(first, 0.0, 1.0)
    keep_n = jnp.where(last, 0.0, 1.0)

    def proj(g):
        return _dot(xb, wm_ref[:, g * gw:(g + 1) * gw])

    def conv_group(g):
        w = wc_ref[:, g * gw:(g + 1) * gw]
        ext = jnp.concatenate([_dot(xpb, w) * keep_p, _dot(xb, w), _dot(xnb, w) * keep_n], axis=0)
        out = cb_ref[:, g * gw:(g + 1) * gw]
        for j in range(4):
            out = out + cw_ref[j:j + 1, g * gw:(g + 1) * gw] * ext[6 + j:6 + j + tm, :]
        return out

    fa_ref[...] = proj(0)
    xc_ref[...] = conv_group(0)
    gg_ref[...] = jax.nn.gelu(proj(1))

    cos2 = jnp.concatenate([cos_ref[...], cos_ref[...]], axis=1)
    sin2 = jnp.concatenate([sin_ref[...], sin_ref[...]], axis=1)
    lane = lax.broadcasted_iota(I32, (tm, gw), 1)
    lo = (lane % HEAD_DIM) < (HEAD_DIM // 2)

    def rope(t):
        rot = jnp.where(lo, pltpu.roll(t, gw - HEAD_DIM // 2, 1), pltpu.roll(t, HEAD_DIM // 2, 1))
        return t * cos2 + rot * sin2

    qkv_att = (rope(proj(2)) * (HEAD_DIM ** -0.5), rope(proj(3)), proj(4))
    nl = gw // LANES
    for j, val in enumerate(qkv_att):
        for c in range(nl):
            scr_ref[j, c] = val[:, c * LANES:(c + 1) * LANES]
    for p, (_, dil) in enumerate(DIL_PAIRS):
        for j in range(3):
            ref = att_refs[3 * p + j]
            if dil == 1:
                ref[...] = qkv_att[j].astype(BF16)
            else:
                for r in range(dil):
                    for c in range(nl):
                        c0 = r * gw + c * LANES
                        ref[:, c0:c0 + LANES] = scr_ref[j, c, pl.ds(r, tm // dil, stride=dil), :].astype(BF16)

    def l2n(t):
        return t * lax.rsqrt(_head_sum(t * t) + RMS_EPS)

    dq_ref[...] = l2n(jax.nn.silu(conv_group(1)))
    dk_ref[...] = l2n(jax.nn.silu(conv_group(2)))
    dv_ref[...] = jax.nn.silu(conv_group(3))
    dsg_ref[...] = jax.nn.silu(proj(5))

    bd = _dot(xb, wm_ref[:, 6 * gw:6 * gw + LANES])
    l128 = lax.broadcasted_iota(I32, (tm, LANES), 1)
    beta = jax.nn.sigmoid(bd)
    g = nalog_ref[...] * jax.nn.softplus(bd + dtb_ref[...])
    nh2 = 2 * N_HEADS_G
    dbg_ref[...] = jnp.where(l128 < nh2, beta, jnp.where(l128 < 2 * nh2, g, 0.0))


def _inproj(x2d, cos_t, sin_t, wm, wc, cw, cb, nalog, dtb, seq_len):
    n, d = x2d.shape
    tm = min(512, seq_len)
    nt = n // tm
    tps = seq_len // tm
    hb = tm // SUBLANES
    gw = GROUP_W
    full = lambda shape: pl.BlockSpec(shape, lambda i: (0,) * len(shape))
    row = lambda w: pl.BlockSpec((tm, w), lambda i: (i, 0))
    att_specs, att_shapes = [], []
    for _, dil in DIL_PAIRS:
        att_specs += [pl.BlockSpec((tm // dil, dil * gw), lambda i: (i, 0))] * 3
        att_shapes += [jax.ShapeDtypeStruct((n // dil, dil * gw), BF16)] * 3
    f32_out = lambda w: jax.ShapeDtypeStruct((n, w), F32)
    out_specs = [row(gw)] * 3 + att_specs + [row(gw)] * 4 + [row(LANES)]
    out_shape = [f32_out(gw)] * 3 + att_shapes + [f32_out(gw)] * 4 + [f32_out(LANES)]
    return pl.pallas_call(
        functools.partial(_inproj_kernel, tiles_per_seq=tps),
        grid=(nt,),
        in_specs=[row(d),
                  pl.BlockSpec((SUBLANES, d), lambda i: (jnp.maximum(i * hb - 1, 0), 0)),
                  pl.BlockSpec((SUBLANES, d), lambda i: (jnp.minimum((i + 1) * hb, nt * hb - 1), 0)),
                  full(wm.shape), full(wc.shape), full(cw.shape), full(cb.shape),
                  row(LANES), row(LANES), full(nalog.shape), full(dtb.shape)],
        out_specs=out_specs,
        out_shape=out_shape,
        scratch_shapes=[pltpu.VMEM((3, gw // LANES, tm, LANES), F32)],
        compiler_params=_cparams(("arbitrary",)),
        name="inproj",
    )(x2d, x2d, x2d, wm, wc, cw, cb, cos_t, sin_t, nalog, dtb)


FOURIER_BLK = 16


def _fourier1_kernel(x_ref, t_ref, o_ref):
    for j in range(x_ref.shape[1]):
        xj = x_ref[:, j, :].astype(BF16)
        o_ref[j] = _dot(t_ref[j], xj)


def _fourier2_kernel(re_ref, im_ref, w2_ref, cc_ref, sc_ref, wf_ref, bf_ref, o_ref, *, scale):
    n2 = re_ref.shape[0]
    for j in range(re_ref.shape[1]):
        a = jnp.concatenate([re_ref[:, j, :], im_ref[:, j, :]], axis=0).astype(BF16)
        z = _dot(w2_ref[...], a)
        zr = z[0:n2].astype(BF16)
        zi = z[n2:2 * n2].astype(BF16)
        f = (_dot(zr, cc_ref[...]) + _dot(zi, sc_ref[...])) * scale
        o_ref[:, j, :] = _dot(f.astype(BF16), wf_ref[...]) + bf_ref[...]


def _dft_tables(seq_len):
    n1 = 1 << (int(math.log2(seq_len)) // 2)
    n2 = seq_len // n1
    k1 = jnp.arange(n1, dtype=I32)
    s1 = jnp.arange(n1, dtype=I32)
    s2 = jnp.arange(n2, dtype=I32)
    m = (k1[None, :, None] * (s1[None, None, :] * n2 + s2[:, None, None])) % seq_len
    ang = m.astype(F32) * (2.0 * math.pi / seq_len)
    t1 = jnp.concatenate([jnp.cos(ang), -jnp.sin(ang)], axis=1).astype(BF16)
    k2 = jnp.arange(n2, dtype=I32)
    m2 = (k2[:, None] * s2[None, :]) % n2
    a2 = m2.astype(F32) * (2.0 * math.pi / n2)
    c2, sn2 = jnp.cos(a2), jnp.sin(a2)
    w2 = jnp.concatenate([jnp.concatenate([c2, sn2], axis=1),
                          jnp.concatenate([-sn2, c2], axis=1)], axis=0).astype(BF16)
    c = jnp.arange(HEAD_DIM, dtype=I32)
    ac = ((c[:, None] * c[None, :]) % HEAD_DIM).astype(F32) * (2.0 * math.pi / HEAD_DIM)
    eye = jnp.eye(N_HEADS_G, dtype=F32)
    cc = jnp.kron(eye, jnp.cos(ac)).astype(BF16)
    sc = jnp.kron(eye, jnp.sin(ac)).astype(BF16)
    return n1, n2, t1, w2, cc, sc


def _block_diag(w):
    h, a, b = w.shape
    eye = jnp.eye(h, dtype=w.dtype)
    return (eye[:, None, :, None] * w[:, :, None, :]).reshape(h * a, h * b)


def _fourier(fa, tables, wf_bd, bf, batch, seq_len):
    n1, n2, t1, w2, cc, sc = tables
    gw = GROUP_W
    x4 = fa.reshape(batch, n1, n2, gw)
    fb = FOURIER_BLK
    a = pl.pallas_call(
        _fourier1_kernel,
        grid=(batch, n2 // fb),
        in_specs=[pl.BlockSpec((None, n1, fb, gw), lambda b, i: (b, 0, i, 0)),
                  pl.BlockSpec((fb, 2 * n1, n1), lambda b, i: (i, 0, 0))],
        out_specs=pl.BlockSpec((None, fb, 2 * n1, gw), lambda b, i: (b, i, 0, 0)),
        out_shape=jax.ShapeDtypeStruct((batch, n2, 2 * n1, gw), F32),
        compiler_params=_cparams(("arbitrary", "arbitrary")),
        name="fourier_stage1",
    )(x4, t1)
    nb = n1 // fb
    full = lambda arr: pl.BlockSpec(arr.shape, lambda b, i: (0,) * arr.ndim)
    y = pl.pallas_call(
        functools.partial(_fourier2_kernel, scale=1.0 / math.sqrt(seq_len * HEAD_DIM)),
        grid=(batch, nb),
        in_specs=[pl.BlockSpec((None, n2, fb, gw), lambda b, i: (b, 0, i, 0)),
                  pl.BlockSpec((None, n2, fb, gw), lambda b, i: (b, 0, nb + i, 0)),
                  full(w2), full(cc), full(sc), full(wf_bd), full(bf)],
        out_specs=pl.BlockSpec((None, n2, fb, gw), lambda b, i: (b, 0, i, 0)),
        out_shape=jax.ShapeDtypeStruct((batch, n2, n1, gw), F32),
        compiler_params=_cparams(("arbitrary", "arbitrary")),
        name="fourier_stage2",
    )(a, a, w2, cc, sc, wf_bd, bf)
    return y.reshape(batch * seq_len, gw)


def _lru_kernel(*refs, rev):
    if rev:
        xc_ref, w_ref, b_ref, lam_ref, hf_ref, gg_ref, o_ref, carry_ref = refs
    else:
        xc_ref, w_ref, b_ref, lam_ref, o_ref, carry_ref = refs
    gw = GROUP_W
    t = xc_ref.shape[0]

    @pl.when(pl.program_id(1) == 0)
    def _():
        carry_ref[...] = jnp.zeros_like(carry_ref)

    xc = xc_ref[...]
    gates = _dot(xc.astype(BF16), w_ref[...]) + b_ref[...]
    r = jax.nn.sigmoid(gates[:, 0:gw])
    ig = jax.nn.sigmoid(gates[:, gw:2 * gw])
    log_a = -LRU_C * r * jax.nn.softplus(-lam_ref[...])
    a = jnp.exp(log_a)
    th = jnp.tanh(log_a)
    u = jnp.sqrt(-2.0 * th / (1.0 - th)) * (ig * xc)

    r8 = lax.broadcasted_iota(I32, (t, gw), 0) % SUBLANES
    for s in (1, 2, 4):
        if rev:
            a_s, u_s, m = pltpu.roll(a, t - s, 0), pltpu.roll(u, t - s, 0), r8 < SUBLANES - s
        else:
            a_s, u_s, m = pltpu.roll(a, s, 0), pltpu.roll(u, s, 0), r8 >= s
        u = jnp.where(m, a * u_s + u, u)
        a = jnp.where(m, a * a_s, a)
    carry = carry_ref[...]
    ng = t // SUBLANES
    blocks = [None] * ng
    for g in (range(ng - 1, -1, -1) if rev else range(ng)):
        sl = slice(g * SUBLANES, (g + 1) * SUBLANES)
        blk = u[sl] + a[sl] * carry
        carry = blk[0:1] if rev else blk[SUBLANES - 1:SUBLANES]
        blocks[g] = blk
    carry_ref[...] = carry
    h = jnp.concatenate(blocks, axis=0)
    if rev:
        o_ref[...] = ((hf_ref[...] + h) * gg_ref[...]).astype(o_ref.dtype)
    else:
        o_ref[...] = h


def _lru(xc, gg, w_gate, b_gate, lam, batch, seq_len):
    n, gw = xc.shape
    t = min(1024, seq_len)
    nt = seq_len // t
    fwd = lambda b, i: (b * nt + i, 0)
    bwd = lambda b, i: (b * nt + (nt - 1 - i), 0)
    outs = None
    for d, imap in ((0, fwd), (1, bwd)):
        par = lambda arr: pl.BlockSpec((None,) + arr.shape[1:], lambda b, i, d=d: (d,) + (0,) * (arr.ndim - 1))
        tile = pl.BlockSpec((t, gw), imap)
        ins = [xc, w_gate, b_gate, lam]
        specs = [tile, par(w_gate), par(b_gate), par(lam)]
        if d == 1:
            ins += [outs, gg]
            specs += [tile, tile]
        outs = pl.pallas_call(
            functools.partial(_lru_kernel, rev=bool(d)),
            grid=(batch, nt),
            in_specs=specs,
            out_specs=tile,
            out_shape=jax.ShapeDtypeStruct((n, gw), BF16 if d else F32),
            scratch_shapes=[pltpu.VMEM((1, gw), F32)],
            compiler_params=_cparams(("arbitrary", "arbitrary")),
            name="lru_bwd" if d else "lru_fwd",
        )(*ins)
    return outs


ATT_R = 64
ATT_QB = 128


def _attn_kernel(q_ref, kp_ref, k_ref, kn_ref, vp_ref, v_ref, vn_ref, o_ref, l_ref, *, sub_len):
    tq = q_ref.shape[0]
    base = pl.program_id(2) * tq
    kext = jnp.concatenate([kp_ref[...], k_ref[...], kn_ref[...]], axis=0)
    vext = jnp.concatenate([vp_ref[...], v_ref[...], vn_ref[...]], axis=0)
    qb = min(ATT_QB, tq)
    kw = qb + 2 * ATT_R
    nh = N_HEADS_G
    nq = tq // qb
    order = [(jq, h) for jq in range(nq) for h in range(nh)]
    hs = lambda h: slice(h * HEAD_DIM, (h + 1) * HEAD_DIM)
    q3 = jnp.stack([q_ref[jq * qb:(jq + 1) * qb, hs(h)] for jq, h in order])
    k3 = jnp.stack([kext[jq * qb:jq * qb + kw, hs(h)] for jq, h in order])
    v3 = jnp.stack([vext[jq * qb:jq * qb + kw, hs(h)] for jq, h in order])
    shape = (nq * nh, qb, kw)
    qi = lax.broadcasted_iota(I32, shape, 1)
    kj = lax.broadcasted_iota(I32, shape, 2)
    jqi = lax.broadcasted_iota(I32, shape, 0) // nh
    kpos = base + jqi * qb + kj - ATT_R
    mask = (jnp.abs(kj - ATT_R - qi) <= ATT_R) & (kpos >= 0) & (kpos < sub_len)
    s = jnp.where(mask, _bmm_nt(q3, k3), NEG)
    m = jnp.max(s, axis=-1, keepdims=True)
    p = jnp.exp(s - m)
    l = jnp.sum(p, axis=-1, keepdims=True)
    o3 = _bmm(p.astype(BF16), v3) / l
    lse3 = jnp.broadcast_to(m + jnp.log(l), (nq * nh, qb, HEAD_DIM))
    for jq in range(nq):
        o_ref[jq * qb:(jq + 1) * qb, :] = jnp.concatenate([o3[jq * nh + h] for h in range(nh)], axis=1)
        l_ref[jq * qb:(jq + 1) * qb, :] = jnp.concatenate([lse3[jq * nh + h] for h in range(nh)], axis=1)


def _attn_pattern(aq, ak, av, dil, batch, seq_len):
    gw = GROUP_W
    sub = seq_len // dil
    q3, k3, v3 = (t.reshape(batch, sub, dil * gw) for t in (aq, ak, av))
    tq = min(1024, sub)
    nq = sub // tq
    hb = tq // ATT_R
    nhb = sub // ATT_R
    main = pl.BlockSpec((None, tq, gw), lambda b, r, i: (b, i, r))
    prev = pl.BlockSpec((None, ATT_R, gw), lambda b, r, i: (b, jnp.maximum(i * hb - 1, 0), r))
    nxt = pl.BlockSpec((None, ATT_R, gw), lambda b, r, i: (b, jnp.minimum((i + 1) * hb, nhb - 1), r))
    o, l = pl.pallas_call(
        functools.partial(_attn_kernel, sub_len=sub),
        grid=(batch, dil, nq),
        in_specs=[main, prev, main, nxt, prev, main, nxt],
        out_specs=[main, main],
        out_shape=[jax.ShapeDtypeStruct((batch, sub, dil * gw), F32)] * 2,
        compiler_params=_cparams(("arbitrary", "arbitrary", "arbitrary")),
        name=f"dilated_attn_d{dil}",
    )(q3, k3, k3, k3, v3, v3, v3)
    return o.reshape(batch * sub, dil * gw), l.reshape(batch * sub, dil * gw)


DN_TILE = 256


def _deltanet_kernel(qf_ref, kf_ref, vf_ref, gf_ref, qb_ref, kb_ref, vb_ref, gb_ref,
                     of_ref, ob_ref, state_ref):
    nbat, t = qf_ref.shape[0], qf_ref.shape[1]
    cl = DN_CHUNK
    nc = t // cl
    nh = N_HEADS_G
    hd = HEAD_DIM
    gw = GROUP_W
    nb = nbat * 2 * nc

    @pl.when(pl.program_id(0) == 0)
    def _():
        state_ref[...] = jnp.zeros_like(state_ref)

    rc = lax.broadcasted_iota(I32, (t, LANES), 0) % cl
    gcs, gts = [], []
    for d, g_ref in ((0, gf_ref), (1, gb_ref)):
        gcl, gtl = [], []
        for bi in range(nbat):
            gc = g_ref[bi]
            s = 1
            while s < cl:
                if d:
                    gc = jnp.where(rc < cl - s, gc + pltpu.roll(gc, t - s, 0), gc)
                else:
                    gc = jnp.where(rc >= s, gc + pltpu.roll(gc, s, 0), gc)
                s *= 2
            gcl.append(gc)
            gtl.append(gc.T)
        gcs.append(gcl)
        gts.append(gtl)

    order = [(bi, d, c) for bi in range(nbat) for d in range(2) for c in range(nc)]
    data = ((qf_ref, kf_ref, vf_ref, gf_ref), (qb_ref, kb_ref, vb_ref, gb_ref))
    rows = lambda c: slice(c * cl, (c + 1) * cl)

    q3 = jnp.stack([data[d][0][bi, rows(c), :] for bi, d, c in order])
    k3 = jnp.stack([data[d][1][bi, rows(c), :] for bi, d, c in order])
    v3 = jnp.stack([data[d][2][bi, rows(c), :] for bi, d, c in order])

    def lane_expand(src_of, col_of):
        blocks = []
        for bi, d, c in order:
            src = src_of(d, bi)[rows(c), :]
            blocks.append(jnp.concatenate(
                [jnp.broadcast_to(src[:, col_of(d, h):col_of(d, h) + 1], (cl, hd)) for h in range(nh)], axis=1))
        return jnp.stack(blocks)

    g_col = lambda d, h: 2 * nh + d * nh + h
    beta3 = lane_expand(lambda d, bi: data[d][3][bi], lambda d, h: d * nh + h)
    gcc3 = lane_expand(lambda d, bi: gcs[d][bi], g_col)
    gcr3 = jnp.stack([jnp.concatenate([gts[d][bi][g_col(d, h):g_col(d, h) + 1, rows(c)] for h in range(nh)], axis=1)
                      for bi, d, c in order])

    ri = lax.broadcasted_iota(I32, (nb, cl, gw), 1)
    ci = lax.broadcasted_iota(I32, (nb, cl, gw), 2) % hd
    isb = (lax.broadcasted_iota(I32, (nb, cl, gw), 0) // nc) % 2 == 1
    r2 = jnp.where(isb, ci, ri)
    c2 = jnp.where(isb, ri, ci)
    incl = r2 >= c2
    strict = r2 > c2
    eye = (ri == ci).astype(F32)

    def block_diag(x):
        n = x.shape[0]
        same_head = (lax.broadcasted_iota(I32, (n, gw, gw), 1) // hd) == (lax.broadcasted_iota(I32, (n, gw, gw), 2) // hd)
        return jnp.where(same_head, jnp.concatenate([x.astype(BF16)] * nh, axis=1), jnp.zeros((), BF16))

    decay = jnp.where(incl, jnp.exp(jnp.where(incl, gcc3 - gcr3, 0.0)), 0.0)
    kb3 = k3 * beta3
    vb3 = v3 * beta3
    qs = q3 * (hd ** -0.5)
    mk = _bmm_nt(jnp.concatenate([kb3, qs], axis=1).astype(BF16), block_diag(k3))
    m = mk[:, 0:cl, :] * jnp.where(strict, decay, 0.0)
    a3 = (mk[:, cl:2 * cl, :] * decay).astype(BF16)
    blk = lambda s: (ri // s) == (ci // s)
    x4 = jnp.where(blk(4), -m, 0.0)
    x4sq = _bmm(x4.astype(BF16), block_diag(x4))
    tm = eye + x4
    tm = tm + _bmm(tm.astype(BF16), block_diag(x4sq))
    for s in (4, 8, 16, 32):
        cm = jnp.where(blk(2 * s) & jnp.logical_not(blk(s)), m, 0.0)
        tc_ = _bmm(cm.astype(BF16), block_diag(tm))
        tm = tm - _bmm(tm.astype(BF16), block_diag(tc_))
    eg = jnp.exp(gcc3)
    tmb = tm.astype(BF16)
    w3 = _bmm(tmb, block_diag(vb3))
    u3 = _bmm(tmb, block_diag(kb3 * eg)).astype(BF16)
    qd3 = (qs * eg).astype(BF16)
    isb1 = (lax.broadcasted_iota(I32, (nb, 1, gw), 0) // nc) % 2 == 1
    gl3 = jnp.where(isb1, gcc3[:, 0:1, :], gcc3[:, cl - 1:cl, :])
    kd3 = k3 * jnp.exp(gl3 - gcc3)
    kdt3 = _bmm_nt(eye.astype(BF16), block_diag(kd3)).astype(BF16)
    egl3 = jnp.exp(gl3)
    uq3 = jnp.concatenate([u3, qd3], axis=1)
    ak3 = jnp.concatenate([a3, kdt3], axis=1)

    def step_bodies(arr, c):
        parts = []
        for bi in range(nbat):
            f0 = bi * 2 * nc + c
            b0 = bi * 2 * nc + nc + (nc - 1 - c)
            parts += [arr[f0:f0 + 1], arr[b0:b0 + 1]]
        return jnp.concatenate(parts, axis=0)

    st = state_ref[...]
    for c in range(nc):
        us = _bmm(step_bodies(uq3, c), block_diag(st))
        v_new = step_bodies(w3, c) - us[:, 0:cl, :]
        av = _bmm(step_bodies(ak3, c), block_diag(v_new))
        o2 = us[:, cl:2 * cl, :] + av[:, 0:cl, :]
        st = st * step_bodies(egl3, c) + av[:, cl:2 * cl, :]
        for bi in range(nbat):
            of_ref[bi, rows(c), :] = o2[2 * bi]
            ob_ref[bi, rows(nc - 1 - c), :] = o2[2 * bi + 1]
    state_ref[...] = st


def _deltanet(dq, dk, dv, dbg, batch, seq_len):
    n, gw = dq.shape
    t = min(DN_TILE, seq_len)
    nt = seq_len // t
    fwd = lambda i: (0, i, 0)
    bwd = lambda i: (0, nt - 1 - i, 0)
    tf, tb = pl.BlockSpec((batch, t, gw), fwd), pl.BlockSpec((batch, t, gw), bwd)
    gf, gb = pl.BlockSpec((batch, t, LANES), fwd), pl.BlockSpec((batch, t, LANES), bwd)
    dq, dk, dv = (a.reshape(batch, seq_len, gw) for a in (dq, dk, dv))
    dbg = dbg.reshape(batch, seq_len, LANES)
    outs = pl.pallas_call(
        _deltanet_kernel,
        grid=(nt,),
        in_specs=[tf, tf, tf, gf, tb, tb, tb, gb],
        out_specs=[tf, tb],
        out_shape=[jax.ShapeDtypeStruct((batch, seq_len, gw), F32)] * 2,
        scratch_shapes=[pltpu.VMEM((2 * batch, HEAD_DIM, GROUP_W), F32)],
        compiler_params=_cparams(("arbitrary",)),
        name="deltanet",
    )(dq, dk, dv, dbg, dq, dk, dv, dbg)
    return [o.reshape(n, gw) for o in outs]


def _outproj_kernel(x_ref, fy_ref, yb_ref, o1_ref, l1_ref, o2_ref, l2_ref, o3_ref, l3_ref,
                    dof_ref, dob_ref, dsg_ref, wo_ref, nw_ref, g_ref, b_ref, rwt_ref,
                    x1_ref, aff_ref, scr_ref, *, alpha):
    gw = GROUP_W
    tm = x_ref.shape[0]
    att = []
    for p, ((_, dil), refs) in enumerate(zip(DIL_PAIRS, ((o1_ref, l1_ref), (o2_ref, l2_ref), (o3_ref, l3_ref)))):
        for j, ref in enumerate(refs):
            if dil == 1:
                att.append(ref[...])
            else:
                nl = gw // LANES
                for r in range(dil):
                    for c in range(nl):
                        c0 = r * gw + c * LANES
                        scr_ref[2 * p + j, c, pl.ds(r, tm // dil, stride=dil), :] = ref[:, c0:c0 + LANES]
                att.append(jnp.concatenate([scr_ref[2 * p + j, c] for c in range(nl)], axis=1))
    o1, l1, o2, l2, o3, l3 = att
    m = jnp.maximum(jnp.maximum(l1, l2), l3)
    w1, w2, w3 = jnp.exp(l1 - m), jnp.exp(l2 - m), jnp.exp(l3 - m)
    yc = (w1 * o1 + w2 * o2 + w3 * o3) / (w1 + w2 + w3)
    od = dof_ref[...] + dob_ref[...]
    ms = _head_sum(od * od) * (1.0 / HEAD_DIM)
    yd = od * lax.rsqrt(ms + RMS_EPS) * nw_ref[...] * dsg_ref[...]
    y = (_dot(fy_ref[...].astype(BF16), wo_ref[0:gw, :])
         + _dot(yb_ref[...].astype(BF16), wo_ref[gw:2 * gw, :])
         + _dot(yc.astype(BF16), wo_ref[2 * gw:3 * gw, :])
         + _dot(yd.astype(BF16), wo_ref[3 * gw:4 * gw, :]))
    x1 = _layer_norm(alpha * x_ref[...] + y, g_ref[...], b_ref[...])
    x1_ref[...] = x1
    logits = _dot_nt(rwt_ref[...], x1.astype(BF16))
    mx = jnp.max(logits, axis=0, keepdims=True)
    e = jnp.exp(logits - mx)
    aff_ref[...] = e / jnp.sum(e, axis=0, keepdims=True)


def _outproj(x2d, parts, wo, nw, g, b, rwt, batch, seq_len, alpha):
    n, d = x2d.shape
    gw = GROUP_W
    tm = min(512, seq_len)
    tps = seq_len // tm
    row = lambda w: pl.BlockSpec((tm, w), lambda i: (i, 0))
    full = lambda arr: pl.BlockSpec(arr.shape, lambda i: (0,) * arr.ndim)
    att_specs = []
    for _, dil in DIL_PAIRS:
        att_specs += [pl.BlockSpec((tm // dil, dil * gw), lambda i: (i, 0))] * 2
    return pl.pallas_call(
        functools.partial(_outproj_kernel, alpha=alpha),
        grid=(n // tm,),
        in_specs=[row(d)] + [row(gw)] * 2 + att_specs + [row(gw)] * 3
                 + [full(wo), full(nw), full(g), full(b), full(rwt)],
        out_specs=[row(d), pl.BlockSpec((None, N_EXPERTS, tm), lambda i: (i // tps, 0, i % tps))],
        out_shape=[jax.ShapeDtypeStruct((n, d), F32),
                   jax.ShapeDtypeStruct((batch, N_EXPERTS, seq_len), F32)],
        scratch_shapes=[pltpu.VMEM((2 * len(DIL_PAIRS), gw // LANES, tm, LANES), F32)],
        compiler_params=_cparams(("arbitrary",)),
        name="outproj_ln_router",
    )(x2d, *parts, wo, nw, g, b, rwt)


def _topk_kernel(a_ref, idx_ref, gate_ref, slot_ref, off_ref, *, cap):
    ng = a_ref.shape[0]
    v = a_ref[...]
    bits = pltpu.bitcast(v, I32)

    thr = jnp.int32(0)
    for shift in range(28, -1, -4):
        n_cand = 7 if shift == 28 else 15
        digit = jnp.int32(0)
        for c in range(1, n_cand + 1):
            cnt = jnp.sum((bits >= (thr | jnp.int32(c << shift))).astype(I32))
            digit = digit + (cnt >= cap).astype(I32)
        thr = thr | (digit << shift)
    gt = bits > thr
    eq = bits == thr
    need_eq = cap - jnp.sum(gt.astype(I32))

    li = lax.broadcasted_iota(I32, (LANES, LANES), 0)
    lj = lax.broadcasted_iota(I32, (LANES, LANES), 1)
    ut_incl = (li <= lj).astype(BF16)
    gi = lax.broadcasted_iota(I32, (ng, ng), 0)
    gj = lax.broadcasted_iota(I32, (ng, ng), 1)
    lt_strict = (gj < gi).astype(BF16)

    def prefix(mask):
        p1 = _dot(mask.astype(BF16), ut_incl)
        tot = jnp.broadcast_to(p1[:, LANES - 1:LANES], (ng, LANES))
        return p1, _dot(lt_strict, tot.astype(BF16)), tot

    p1e, offe, _ = prefix(eq)
    rank_eq = p1e - eq.astype(F32) + offe
    sel = gt | (eq & (rank_eq < need_eq.astype(F32)))
    p1, offs, tot = prefix(sel)
    slot_ref[...] = jnp.where(sel, (p1 + offs).astype(I32) - 1, -1)
    off_ref[...] = offs.astype(I32)

    j = lax.broadcasted_iota(I32, (cap, 1), 0).astype(F32)
    ends_row = (offs + tot).T[0:1, :]
    offs_row = offs.T[0:1, :]
    gj_ = jnp.sum((ends_row <= j).astype(I32), axis=-1, keepdims=True)
    oh = lax.broadcasted_iota(I32, (cap, ng), 1) == gj_
    off_j = jnp.sum(jnp.where(oh, offs_row, 0.0), axis=-1, keepdims=True)
    ohb = oh.astype(BF16)
    prow = _dot(ohb, p1.astype(BF16))
    lo = jnp.sum((prow <= (j - off_j)).astype(I32), axis=-1, keepdims=True)
    idx_ref[...] = gj_ * LANES + lo
    h1 = v.astype(BF16)
    r1 = v - h1.astype(F32)
    h2 = r1.astype(BF16)
    h3 = (r1 - h2.astype(F32)).astype(BF16)
    arow = _dot(ohb, h1) + _dot(ohb, h2) + _dot(ohb, h3)
    lane = lax.broadcasted_iota(I32, (cap, LANES), 1)
    gate_ref[...] = jnp.sum(jnp.where(lane == lo, arow, 0.0), axis=-1, keepdims=True)


def _topk(aff_t, cap):
    batch, ne, seq_len = aff_t.shape
    ng = seq_len // LANES
    a4 = aff_t.reshape(batch, ne, ng, LANES)
    grp = pl.BlockSpec((None, None, ng, LANES), lambda b, e: (b, e, 0, 0))
    col = pl.BlockSpec((None, None, cap, 1), lambda b, e: (b, e, 0, 0))
    return pl.pallas_call(
        functools.partial(_topk_kernel, cap=cap),
        grid=(batch, ne),
        in_specs=[grp],
        out_specs=[col, col, grp, grp],
        out_shape=[jax.ShapeDtypeStruct((batch, ne, cap, 1), I32),
                   jax.ShapeDtypeStruct((batch, ne, cap, 1), F32),
                   jax.ShapeDtypeStruct((batch, ne, ng, LANES), I32),
                   jax.ShapeDtypeStruct((batch, ne, ng, LANES), I32)],
        compiler_params=_cparams(("arbitrary", "arbitrary")),
        name="expert_topk",
    )(a4)


FFN_ROWS = 512


def _ffn_kernel(idx_ref, idxn_ref, x_hbm, gate_ref, w1_ref, w3_ref, w2_ref, y_hbm,
                xg_ref, gbuf_ref, acc_ref, sem, osem, *, seq_len):
    b = pl.program_id(0)
    e = pl.program_id(1)
    f = pl.program_id(2)
    ne = pl.num_programs(1)
    blk_id = b * ne + e
    n_blocks = pl.num_programs(0) * ne
    cap = xg_ref.shape[0]
    rb = min(FFN_ROWS, cap)

    def row_copy(iref, base, r):
        return pltpu.make_async_copy(x_hbm.at[pl.ds(base + iref[0, 0, r], 1)], gbuf_ref.at[pl.ds(r, 1)], sem)

    def gather_start(iref, base):
        def body(r, c):
            row_copy(iref, base, r).start()
            return c
        lax.fori_loop(0, cap, body, 0, unroll=8)

    @pl.when(f == 0)
    def _():
        @pl.when(blk_id == 0)
        def _():
            gather_start(idx_ref, b * seq_len)

        def wait_body(r, c):
            row_copy(idx_ref, b * seq_len, r).wait()
            return c
        lax.fori_loop(0, cap, wait_body, 0, unroll=8)
        for blk in range(cap // rb):
            rs = slice(blk * rb, (blk + 1) * rb)
            xg_ref[rs, :] = gbuf_ref[rs, :].astype(BF16)
            acc_ref[rs, :] = jnp.zeros((rb, acc_ref.shape[1]), F32)

        @pl.when(blk_id + 1 < n_blocks)
        def _():
            gather_start(idxn_ref, ((blk_id + 1) // ne) * seq_len)

    fc = w1_ref.shape[1]
    w13 = jnp.concatenate([w1_ref[...].astype(BF16), w3_ref[...].astype(BF16)], axis=1)
    w2 = w2_ref[...].astype(BF16)
    for blk in range(cap // rb):
        rs = slice(blk * rb, (blk + 1) * rb)
        h13 = _dot(xg_ref[rs, :], w13)
        h = (jax.nn.silu(h13[:, 0:fc]) * h13[:, fc:2 * fc]).astype(BF16)
        acc_ref[rs, :] += _dot(h, w2)

    @pl.when(f == pl.num_programs(2) - 1)
    def _():
        for blk in range(cap // rb):
            rs = slice(blk * rb, (blk + 1) * rb)
            xg_ref[rs, :] = (acc_ref[rs, :] * gate_ref[rs, :]).astype(BF16)
        out_copy = pltpu.make_async_copy(xg_ref, y_hbm.at[b, e], osem)
        out_copy.start()
        out_copy.wait()


def _ffn(x1, idx, gate, w1, w3, w2, layer, batch, seq_len):
    n, d = x1.shape
    ne, cap = idx.shape[1], idx.shape[2]
    dff = w1.shape[-1]
    fc = 896 if dff % 896 == 0 else dff
    nf = dff // fc
    nblk = batch * ne
    idx3 = idx.reshape(nblk, 1, cap)
    return pl.pallas_call(
        functools.partial(_ffn_kernel, seq_len=seq_len),
        grid=(batch, ne, nf),
        in_specs=[pl.BlockSpec((1, 1, cap), lambda b, e, f: (b * ne + e, 0, 0), memory_space=pltpu.SMEM),
                  pl.BlockSpec((1, 1, cap), lambda b, e, f: (jnp.minimum(b * ne + e + 1, nblk - 1), 0, 0),
                               memory_space=pltpu.SMEM),
                  pl.BlockSpec(memory_space=pl.ANY),
                  pl.BlockSpec((None, None, cap, 1), lambda b, e, f: (b, e, 0, 0)),
                  pl.BlockSpec((None, None, d, fc), lambda b, e, f: (layer, e, 0, f)),
                  pl.BlockSpec((None, None, d, fc), lambda b, e, f: (layer, e, 0, f)),
                  pl.BlockSpec((None, None, fc, d), lambda b, e, f: (layer, e, f, 0))],
        out_specs=pl.BlockSpec(memory_space=pl.ANY),
        out_shape=jax.ShapeDtypeStruct((batch, ne, cap, d), BF16),
        scratch_shapes=[pltpu.VMEM((cap, d), BF16),
                        pltpu.VMEM((cap, d), F32),
                        pltpu.VMEM((cap, d), F32),
                        pltpu.SemaphoreType.DMA(()),
                        pltpu.SemaphoreType.DMA(())],
        compiler_params=_cparams(("arbitrary",) * 3),
        name="expert_ffn",
    )(idx3, idx3, x1, gate, w1, w3, w2)


CMB_WIN = 64
CMB_ALIGN = 16


def _combine_kernel(offs_ref, x_ref, slot_ref, y_hbm, g_ref, b_ref, o_ref, buf_ref, buf2_ref, acc_ref,
                    sem, sem2, *, alpha, cap, tiles_per_seq):
    i = pl.program_id(0)
    nsteps = pl.num_programs(0)
    tc = x_ref.shape[0]
    ne = N_EXPERTS
    kw = ne * CMB_WIN

    def tile(step):
        bidx = step // tiles_per_seq
        return bidx, (bidx * (tiles_per_seq + 1) + step % tiles_per_seq) * ne

    def window(base, e, k):
        start = (offs_ref[base + e] // CMB_ALIGN) * CMB_ALIGN + k * CMB_WIN
        return pl.multiple_of(jnp.minimum(start, cap - CMB_WIN), CMB_ALIGN)

    def first_copy(step, buf, e):
        bidx, base = tile(step)
        return pltpu.make_async_copy(y_hbm.at[bidx, e, pl.ds(window(base, e, 0), CMB_WIN)],
                                     buf_ref.at[buf, pl.ds(e * CMB_WIN, CMB_WIN)], sem.at[buf, e])

    cur = i % 2

    @pl.when(i == 0)
    def _():
        for e in range(ne):
            first_copy(i, cur, e).start()

    @pl.when(i + 1 < nsteps)
    def _():
        for e in range(ne):
            first_copy(i + 1, 1 - cur, e).start()

    bidx, base = tile(i)
    slots = slot_ref[...]
    sp = slots + 1
    el = lax.broadcasted_iota(I32, (ne, kw), 1) // CMB_WIN
    expand = (el == lax.broadcasted_iota(I32, (ne, kw), 0)).astype(BF16)
    sp_exp = (_dot((sp >> 6).astype(F32).astype(BF16), expand) * 64.0
              + _dot((sp & 63).astype(F32).astype(BF16), expand))
    lane1 = lax.broadcasted_iota(I32, (1, kw), 1)
    win_row = jnp.zeros((1, kw), I32)
    for e in range(ne):
        win_row = jnp.where(lane1 // CMB_WIN == e, window(base, e, 0) + 1, win_row)
    oh = (sp_exp - win_row.astype(F32)) == (lane1 % CMB_WIN).astype(F32)
    for e in range(ne):
        first_copy(i, cur, e).wait()
    acc_ref[...] = alpha * x_ref[...] + _dot(oh.astype(BF16), buf_ref[cur])

    def n_windows(e):
        off = offs_ref[base + e]
        end = offs_ref[base + ne + e]
        return (end - (off // CMB_ALIGN) * CMB_ALIGN + CMB_WIN - 1) // CMB_WIN

    overflow = n_windows(0) > 1
    for e in range(1, ne):
        overflow = overflow | (n_windows(e) > 1)

    @pl.when(overflow)
    def _():
        lane = lax.broadcasted_iota(I32, (tc, CMB_WIN), 1)
        for e in range(ne):
            slot = slots[:, e:e + 1]

            def extra(k, c):
                w0 = window(base, e, k)
                cp = pltpu.make_async_copy(y_hbm.at[bidx, e, pl.ds(w0, CMB_WIN)], buf2_ref, sem2)
                cp.start()
                cp.wait()
                ohk = ((slot - w0) == lane) & ((slot - window(base, e, 0)) >= k * CMB_WIN)
                acc_ref[...] += _dot(ohk.astype(BF16), buf2_ref[...])
                return c

            lax.fori_loop(1, jnp.maximum(n_windows(e), 1), extra, 0)

    o_ref[...] = _layer_norm(acc_ref[...], g_ref[...], b_ref[...])


def _combine(x1, y, slot_tm, offs, g, b, batch, seq_len, alpha):
    n, d = x1.shape
    ne, cap = y.shape[1], y.shape[2]
    tc = min(256, seq_len)
    tps = seq_len // tc
    return pl.pallas_call(
        functools.partial(_combine_kernel, alpha=alpha, cap=cap, tiles_per_seq=tps),
        grid_spec=pltpu.PrefetchScalarGridSpec(
            num_scalar_prefetch=1,
            grid=(n // tc,),
            in_specs=[pl.BlockSpec((tc, d), lambda i, o: (i, 0)),
                      pl.BlockSpec((tc, ne), lambda i, o: (i, 0)),
                      pl.BlockSpec(memory_space=pl.ANY),
                      pl.BlockSpec((1, d), lambda i, o: (0, 0)),
                      pl.BlockSpec((1, d), lambda i, o: (0, 0))],
            out_specs=pl.BlockSpec((tc, d), lambda i, o: (i, 0)),
            scratch_shapes=[pltpu.VMEM((2, ne * CMB_WIN, d), BF16),
                            pltpu.VMEM((CMB_WIN, d), BF16),
                            pltpu.VMEM((tc, d), F32),
                            pltpu.SemaphoreType.DMA((2, ne)),
                            pltpu.SemaphoreType.DMA(())]),
        out_shape=jax.ShapeDtypeStruct((n, d), F32),
        compiler_params=_cparams(("arbitrary",)),
        name="combine_ln",
    )(offs, x1, slot_tm, y, g, b)


def _moe(x1, aff_t, w1, w3, w2, g2, b2, layer, batch, seq_len, alpha):
    cap = EC_FACTOR * seq_len // N_EXPERTS
    idx, gate, slot, offs = _topk(aff_t, cap)
    y = _ffn(x1, idx[..., 0], gate, w1, w3, w2, layer, batch, seq_len)
    tc = min(256, seq_len)
    slot_tm = jnp.transpose(slot.reshape(batch, N_EXPERTS, seq_len), (0, 2, 1)).reshape(batch * seq_len, N_EXPERTS)
    tile_off = offs[:, :, ::tc // LANES, 0]
    tile_off = jnp.concatenate([tile_off, jnp.full((batch, N_EXPERTS, 1), cap, I32)], axis=2)
    tile_off = jnp.transpose(tile_off, (0, 2, 1)).reshape(-1)
    return _combine(x1, y, slot_tm, tile_off, g2, b2, batch, seq_len, alpha)


def kernel(x, positions, w_in, w_out, fno_w, fno_b, lru_conv_w, lru_conv_b, lru_wa, lru_ba, lru_wx, lru_bx,
           lru_lam, dn_conv_w, dn_conv_b, dn_a_log, dn_dt_bias, dn_norm_w, ln1_g, ln1_b, router_w,
           exp_w1, exp_w3, exp_w2, ln2_g, ln2_b):
    batch, seq_len, d = x.shape
    depth = w_in.shape[0]
    gw = GROUP_W
    nh2 = 2 * N_HEADS_G
    alpha = (2.0 * depth) ** 0.25
    n = batch * seq_len

    cos_t, sin_t = _rope_tables(positions)
    tables = _dft_tables(seq_len)

    col = lambda a, b: w_in[:, :, a * gw:b * gw]
    bd = jnp.pad(w_in[:, :, 10 * gw:], ((0, 0), (0, 0), (0, LANES - 2 * nh2)))
    wm = jnp.concatenate([col(0, 1), col(2, 3), col(3, 6), col(9, 10), bd], axis=2).astype(BF16)
    wc = jnp.concatenate([col(1, 2), col(6, 9)], axis=2).astype(BF16)
    cw = jnp.concatenate([lru_conv_w, dn_conv_w], axis=2)
    cb = jnp.concatenate([lru_conv_b, dn_conv_b], axis=1)[:, None, :]
    pad16 = lambda v: jnp.pad(v.reshape(depth, 1, nh2), ((0, 0), (0, 0), (nh2, LANES - 2 * nh2)))
    nalog = pad16(-jnp.exp(dn_a_log.astype(F32)))
    dtb = pad16(dn_dt_bias)
    wf_bd = jax.vmap(_block_diag)(fno_w).astype(BF16)
    bf = fno_b.reshape(depth, 1, gw)
    lru_w = jnp.concatenate([jax.vmap(jax.vmap(_block_diag))(lru_wa),
                             jax.vmap(jax.vmap(_block_diag))(lru_wx)], axis=3).astype(BF16)
    lru_b = jnp.concatenate([lru_ba, lru_bx], axis=2)[:, :, None, :]
    lam = lru_lam[:, :, None, :]
    wo = w_out.astype(BF16)
    nw = jnp.tile(dn_norm_w, (1, N_HEADS_G))[:, None, :]
    rwt = jnp.transpose(router_w, (0, 2, 1)).astype(BF16)

    x2d = x.reshape(n, d)
    for l in range(depth):
        outs = _inproj(x2d, cos_t, sin_t, wm[l], wc[l], cw[l], cb[l], nalog[l], dtb[l], seq_len)
        fa, xc, gg = outs[:3]
        n_att = 3 * len(DIL_PAIRS)
        att_in = outs[3:3 + n_att]
        dq, dk, dv, dsg, dbg = outs[3 + n_att:]
        fy = _fourier(fa, tables, wf_bd[l], bf[l], batch, seq_len)
        yb = _lru(xc, gg, lru_w[l], lru_b[l], lam[l], batch, seq_len)
        att = []
        for p, (window, dil) in enumerate(DIL_PAIRS):
            assert (window // 2) // dil == ATT_R
            att += list(_attn_pattern(*att_in[3 * p:3 * p + 3], dil, batch, seq_len))
        dof, dob = _deltanet(dq, dk, dv, dbg, batch, seq_len)
        x1, aff_t = _outproj(x2d, [fy, yb] + att + [dof, dob, dsg], wo[l], nw[l],
                             ln1_g[l][None, :], ln1_b[l][None, :], rwt[l], batch, seq_len, alpha)
        x2d = _moe(x1, aff_t, exp_w1, exp_w3, exp_w2, ln2_g[l][None, :], ln2_b[l][None, :],
                   l, batch, seq_len, alpha)
    return x2d.reshape(batch, seq_len, d)
```

```python
import functools
import math

import jax
import jax.numpy as jnp
from jax import lax
from jax.experimental import pallas as pl
from jax.experimental.pallas import tpu as pltpu

F32 = jnp.float32
BF16 = jnp.bfloat16
I32 = jnp.int32

GROUP_W = 256
HEAD_DIM = 64
N_HEADS_G = 4
LRU_C = 8.0
DIL_PAIRS = ((128, 1), (512, 4), (2048, 16))
ROPE_THETA = 10000.0
DN_CHUNK = 64
N_EXPERTS = 16
EC_FACTOR = 2
LN_EPS = 1e-5
RMS_EPS = 1e-6
NEG = -1e30

LANES = 128
SUBLANES = 8
VMEM_LIMIT = 56 * 1024 * 1024


def _cparams(sem):
    return pltpu.CompilerParams(dimension_semantics=sem, vmem_limit_bytes=VMEM_LIMIT)


def _dot(a, b):
    return jnp.dot(a, b, preferred_element_type=F32)


def _dot_nt(a, b):
    return lax.dot_general(a, b, (((1,), (1,)), ((), ())), preferred_element_type=F32)


def _bmm(a, b):
    return jnp.einsum('bij,bjk->bik', a, b, preferred_element_type=F32)


def _bmm_nt(a, b):
    return jnp.einsum('bij,bkj->bik', a, b, preferred_element_type=F32)


def _head_sum(t):
    lane = lax.broadcasted_iota(I32, t.shape, 1)
    head = lane // HEAD_DIM
    out = jnp.zeros_like(t)
    for h in range(N_HEADS_G):
        m = head == h
        s = jnp.sum(jnp.where(m, t, 0.0), axis=-1, keepdims=True)
        out = jnp.where(m, s, out)
    return out


def _layer_norm(v, g, b):
    mu = jnp.mean(v, axis=-1, keepdims=True)
    c = v - mu
    var = jnp.mean(c * c, axis=-1, keepdims=True)
    return c * lax.rsqrt(var + LN_EPS) * g + b


def _rope_table_kernel(pos_ref, inv_ref, sgn_ref, cos_ref, sin_ref):
    ang = pos_ref[...] * inv_ref[...]
    cos_ref[...] = jnp.cos(ang)
    sin_ref[...] = jnp.sin(ang) * sgn_ref[...]


def _rope_tables(positions):
    n = positions.size
    half = HEAD_DIM // 2
    inv = ROPE_THETA ** (-jnp.arange(half, dtype=F32) / half)
    inv = jnp.tile(inv, LANES // half)[None, :]
    lane = jnp.arange(LANES)
    sgn = jnp.where((lane % HEAD_DIM) < half, -1.0, 1.0).astype(F32)[None, :]
    pos = positions.reshape(n, 1).astype(F32)
    tm = min(2048, n)
    return pl.pallas_call(
        _rope_table_kernel,
        grid=(n // tm,),
        in_specs=[pl.BlockSpec((tm, 1), lambda i: (i, 0)),
                  pl.BlockSpec((1, LANES), lambda i: (0, 0)),
                  pl.BlockSpec((1, LANES), lambda i: (0, 0))],
        out_specs=[pl.BlockSpec((tm, LANES), lambda i: (i, 0))] * 2,
        out_shape=[jax.ShapeDtypeStruct((n, LANES), F32)] * 2,
        compiler_params=_cparams(("arbitrary",)),
        name="rope_tables",
    )(pos, inv, sgn)


def _inproj_kernel(x_ref, xp_ref, xn_ref, wm_ref, wc_ref, cw_ref, cb_ref, cos_ref, sin_ref,
                   nalog_ref, dtb_ref,
                   fa_ref, xc_ref, gg_ref, *rest, tiles_per_seq):
    n_att = 3 * len(DIL_PAIRS)
    att_refs = rest[:n_att]
    dq_ref, dk_ref, dv_ref, dsg_ref, dbg_ref, scr_ref = rest[n_att:]
    i = pl.program_id(0)
    tm = x_ref.shape[0]
    gw = GROUP_W
    first = (i % tiles_per_seq) == 0
    last = (i % tiles_per_seq) == tiles_per_seq - 1
    xb = x_ref[...].astype(BF16)
    xpb = xp_ref[...].astype(BF16)
    xnb = xn_ref[...].astype(BF16)
    keep_p = jnp.where(first, 0.0, 1.0)
    keep_n = jnp.where(last, 0.0, 1.0)

    def proj(g):
        return _dot(xb, wm_ref[:, g * gw:(g + 1) * gw])

    def conv_group(g):
        w = wc_ref[:, g * gw:(g + 1) * gw]
        ext = jnp.concatenate([_dot(xpb, w) * keep_p, _dot(xb, w), _dot(xnb, w) * keep_n], axis=0)
        out = cb_ref[:, g * gw:(g + 1) * gw]
        for j in range(4):
            out = out + cw_ref[j:j + 1, g * gw:(g + 1) * gw] * ext[6 + j:6 + j + tm, :]
        return out

    fa_ref[...] = proj(0)
    xc_ref[...] = conv_group(0)
    gg_ref[...] = jax.nn.gelu(proj(1))

    cos2 = jnp.concatenate([cos_ref[...], cos_ref[...]], axis=1)
    sin2 = jnp.concatenate([sin_ref[...], sin_ref[...]], axis=1)
    lane = lax.broadcasted_iota(I32, (tm, gw), 1)
    lo = (lane % HEAD_DIM) < (HEAD_DIM // 2)

    def rope(t):
        rot = jnp.where(lo, pltpu.roll(t, gw - HEAD_DIM // 2, 1), pltpu.roll(t, HEAD_DIM // 2, 1))
        return t * cos2 + rot * sin2

    qkv_att = (rope(proj(2)) * (HEAD_DIM ** -0.5), rope(proj(3)), proj(4))
    nl = gw // LANES
    for j, val in enumerate(qkv_att):
        for c in range(nl):
            scr_ref[j, c] = val[:, c * LANES:(c + 1) * LANES]
    for p, (_, dil) in enumerate(DIL_PAIRS):
        for j in range(3):
            ref = att_refs[3 * p + j]
            if dil == 1:
                ref[...] = qkv_att[j].astype(BF16)
            else:
                for r in range(dil):
                    for c in range(nl):
                        c0 = r * gw + c * LANES
                        ref[:, c0:c0 + LANES] = scr_ref[j, c, pl.ds(r, tm // dil, stride=dil), :].astype(BF16)

    def l2n(t):
        return t * lax.rsqrt(_head_sum(t * t) + RMS_EPS)

    dq_ref[...] = l2n(jax.nn.silu(conv_group(1)))
    dk_ref[...] = l2n(jax.nn.silu(conv_group(2)))
    dv_ref[...] = jax.nn.silu(conv_group(3))
    dsg_ref[...] = jax.nn.silu(proj(5))

    bd = _dot(xb, wm_ref[:, 6 * gw:6 * gw + LANES])
    l128 = lax.broadcasted_iota(I32, (tm, LANES), 1)
    beta = jax.nn.sigmoid(bd)
    g = nalog_ref[...] * jax.nn.softplus(bd + dtb_ref[...])
    nh2 = 2 * N_HEADS_G
    dbg_ref[...] = jnp.where(l128 < nh2, beta, jnp.where(l128 < 2 * nh2, g, 0.0))


def _inproj(x2d, cos_t, sin_t, wm, wc, cw, cb, nalog, dtb, seq_len):
    n, d = x2d.shape
    tm = min(512, seq_len)
    nt = n // tm
    tps = seq_len // tm
    hb = tm // SUBLANES
    gw = GROUP_W
    full = lambda shape: pl.BlockSpec(shape, lambda i: (0,) * len(shape))
    row = lambda w: pl.BlockSpec((tm, w), lambda i: (i, 0))
    att_specs, att_shapes = [], []
    for _, dil in DIL_PAIRS:
        att_specs += [pl.BlockSpec((tm // dil, dil * gw), lambda i: (i, 0))] * 3
        att_shapes += [jax.ShapeDtypeStruct((n // dil, dil * gw), BF16)] * 3
    f32_out = lambda w: jax.ShapeDtypeStruct((n, w), F32)
    out_specs = [row(gw)] * 3 + att_specs + [row(gw)] * 4 + [row(LANES)]
    out_shape = [f32_out(gw)] * 3 + att_shapes + [f32_out(gw)] * 4 + [f32_out(LANES)]
    return pl.pallas_call(
        functools.partial(_inproj_kernel, tiles_per_seq=tps),
        grid=(nt,),
        in_specs=[row(d),
                  pl.BlockSpec((SUBLANES, d), lambda i: (jnp.maximum(i * hb - 1, 0), 0)),
                  pl.BlockSpec((SUBLANES, d), lambda i: (jnp.minimum((i + 1) * hb, nt * hb - 1), 0)),
                  full(wm.shape), full(wc.shape), full(cw.shape), full(cb.shape),
                  row(LANES), row(LANES), full(nalog.shape), full(dtb.shape)],
        out_specs=out_specs,
        out_shape=out_shape,
        scratch_shapes=[pltpu.VMEM((3, gw // LANES, tm, LANES), F32)],
        compiler_params=_cparams(("arbitrary",)),
        name="inproj",
    )(x2d, x2d, x2d, wm, wc, cw, cb, cos_t, sin_t, nalog, dtb)


FOURIER_BLK = 16


def _fourier1_kernel(x_ref, t_ref, o_ref):
    for j in range(x_ref.shape[1]):
        xj = x_ref[:, j, :].astype(BF16)
        o_ref[j] = _dot(t_ref[j], xj)


def _fourier2_kernel(re_ref, im_ref, w2_ref, cc_ref, sc_ref, wf_ref, bf_ref, o_ref, *, scale):
    n2 = re_ref.shape[0]
    for j in range(re_ref.shape[1]):
        a = jnp.concatenate([re_ref[:, j, :], im_ref[:, j, :]], axis=0).astype(BF16)
        z = _dot(w2_ref[...], a)
        zr = z[0:n2].astype(BF16)
        zi = z[n2:2 * n2].astype(BF16)
        f = (_dot(zr, cc_ref[...]) + _dot(zi, sc_ref[...])) * scale
        o_ref[:, j, :] = _dot(f.astype(BF16), wf_ref[...]) + bf_ref[...]


def _dft_tables(seq_len):
    n1 = 1 << (int(math.log2(seq_len)) // 2)
    n2 = seq_len // n1
    k1 = jnp.arange(n1, dtype=I32)
    s1 = jnp.arange(n1, dtype=I32)
    s2 = jnp.arange(n2, dtype=I32)
    m = (k1[None, :, None] * (s1[None, None, :] * n2 + s2[:, None, None])) % seq_len
    ang = m.astype(F32) * (2.0 * math.pi / seq_len)
    t1 = jnp.concatenate([jnp.cos(ang), -jnp.sin(ang)], axis=1).astype(BF16)
    k2 = jnp.arange(n2, dtype=I32)
    m2 = (k2[:, None] * s2[None, :]) % n2
    a2 = m2.astype(F32) * (2.0 * math.pi / n2)
    c2, sn2 = jnp.cos(a2), jnp.sin(a2)
    w2 = jnp.concatenate([jnp.concatenate([c2, sn2], axis=1),
                          jnp.concatenate([-sn2, c2], axis=1)], axis=0).astype(BF16)
    c = jnp.arange(HEAD_DIM, dtype=I32)
    ac = ((c[:, None] * c[None, :]) % HEAD_DIM).astype(F32) * (2.0 * math.pi / HEAD_DIM)
    eye = jnp.eye(N_HEADS_G, dtype=F32)
    cc = jnp.kron(eye, jnp.cos(ac)).astype(BF16)
    sc = jnp.kron(eye, jnp.sin(ac)).astype(BF16)
    return n1, n2, t1, w2, cc, sc


def _block_diag(w):
    h, a, b = w.shape
    eye = jnp.eye(h, dtype=w.dtype)
    return (eye[:, None, :, None] * w[:, :, None, :]).reshape(h * a, h * b)


def _fourier(fa, tables, wf_bd, bf, batch, seq_len):
    n1, n2, t1, w2, cc, sc = tables
    gw = GROUP_W
    x4 = fa.reshape(batch, n1, n2, gw)
    fb = FOURIER_BLK
    a = pl.pallas_call(
        _fourier1_kernel,
        grid=(batch, n2 // fb),
        in_specs=[pl.BlockSpec((None, n1, fb, gw), lambda b, i: (b, 0, i, 0)),
                  pl.BlockSpec((fb, 2 * n1, n1), lambda b, i: (i, 0, 0))],
        out_specs=pl.BlockSpec((None, fb, 2 * n1, gw), lambda b, i: (b, i, 0, 0)),
        out_shape=jax.ShapeDtypeStruct((batch, n2, 2 * n1, gw), F32),
        compiler_params=_cparams(("arbitrary", "arbitrary")),
        name="fourier_stage1",
    )(x4, t1)
    nb = n1 // fb
    full = lambda arr: pl.BlockSpec(arr.shape, lambda b, i: (0,) * arr.ndim)
    y = pl.pallas_call(
        functools.partial(_fourier2_kernel, scale=1.0 / math.sqrt(seq_len * HEAD_DIM)),
        grid=(batch, nb),
        in_specs=[pl.BlockSpec((None, n2, fb, gw), lambda b, i: (b, 0, i, 0)),
                  pl.BlockSpec((None, n2, fb, gw), lambda b, i: (b, 0, nb + i, 0)),
                  full(w2), full(cc), full(sc), full(wf_bd), full(bf)],
        out_specs=pl.BlockSpec((None, n2, fb, gw), lambda b, i: (b, 0, i, 0)),
        out_shape=jax.ShapeDtypeStruct((batch, n2, n1, gw), F32),
        compiler_params=_cparams(("arbitrary", "arbitrary")),
        name="fourier_stage2",
    )(a, a, w2, cc, sc, wf_bd, bf)
    return y.reshape(batch * seq_len, gw)


def _lru_kernel(*refs, rev):
    if rev:
        xc_ref, w_ref, b_ref, lam_ref, hf_ref, gg_ref, o_ref, carry_ref = refs
    else:
        xc_ref, w_ref, b_ref, lam_ref, o_ref, carry_ref = refs
    gw = GROUP_W
    t = xc_ref.shape[0]

    @pl.when(pl.program_id(1) == 0)
    def _():
        carry_ref[...] = jnp.zeros_like(carry_ref)

    xc = xc_ref[...]
    gates = _dot(xc.astype(BF16), w_ref[...]) + b_ref[...]
    r = jax.nn.sigmoid(gates[:, 0:gw])
    ig = jax.nn.sigmoid(gates[:, gw:2 * gw])
    log_a = -LRU_C * r * jax.nn.softplus(-lam_ref[...])
    a = jnp.exp(log_a)
    th = jnp.tanh(log_a)
    u = jnp.sqrt(-2.0 * th / (1.0 - th)) * (ig * xc)

    r8 = lax.broadcasted_iota(I32, (t, gw), 0) % SUBLANES
    for s in (1, 2, 4):
        if rev:
            a_s, u_s, m = pltpu.roll(a, t - s, 0), pltpu.roll(u, t - s, 0), r8 < SUBLANES - s
        else:
            a_s, u_s, m = pltpu.roll(a, s, 0), pltpu.roll(u, s, 0), r8 >= s
        u = jnp.where(m, a * u_s + u, u)
        a = jnp.where(m, a * a_s, a)
    carry = carry_ref[...]
    ng = t // SUBLANES
    blocks = [None] * ng
    for g in (range(ng - 1, -1, -1) if rev else range(ng)):
        sl = slice(g * SUBLANES, (g + 1) * SUBLANES)
        blk = u[sl] + a[sl] * carry
        carry = blk[0:1] if rev else blk[SUBLANES - 1:SUBLANES]
        blocks[g] = blk
    carry_ref[...] = carry
    h = jnp.concatenate(blocks, axis=0)
    if rev:
        o_ref[...] = ((hf_ref[...] + h) * gg_ref[...]).astype(o_ref.dtype)
    else:
        o_ref[...] = h


def _lru(xc, gg, w_gate, b_gate, lam, batch, seq_len):
    n, gw = xc.shape
    t = min(1024, seq_len)
    nt = seq_len // t
    fwd = lambda b, i: (b * nt + i, 0)
    bwd = lambda b, i: (b * nt + (nt - 1 - i), 0)
    outs = None
    for d, imap in ((0, fwd), (1, bwd)):
        par = lambda arr: pl.BlockSpec((None,) + arr.shape[1:], lambda b, i, d=d: (d,) + (0,) * (arr.ndim - 1))
        tile = pl.BlockSpec((t, gw), imap)
        ins = [xc, w_gate, b_gate, lam]
        specs = [tile, par(w_gate), par(b_gate), par(lam)]
        if d == 1:
            ins += [outs, gg]
            specs += [tile, tile]
        outs = pl.pallas_call(
            functools.partial(_lru_kernel, rev=bool(d)),
            grid=(batch, nt),
            in_specs=specs,
            out_specs=tile,
            out_shape=jax.ShapeDtypeStruct((n, gw), BF16 if d else F32),
            scratch_shapes=[pltpu.VMEM((1, gw), F32)],
            compiler_params=_cparams(("arbitrary", "arbitrary")),
            name="lru_bwd" if d else "lru_fwd",
        )(*ins)
    return outs


ATT_R = 64
ATT_QB = 128


def _attn_kernel(q_ref, kp_ref, k_ref, kn_ref, vp_ref, v_ref, vn_ref, o_ref, l_ref, *, sub_len):
    tq = q_ref.shape[0]
    base = pl.program_id(2) * tq
    kext = jnp.concatenate([kp_ref[...], k_ref[...], kn_ref[...]], axis=0)
    vext = jnp.concatenate([vp_ref[...], v_ref[...], vn_ref[...]], axis=0)
    qb = min(ATT_QB, tq)
    kw = qb + 2 * ATT_R
    nh = N_HEADS_G
    nq = tq // qb
    order = [(jq, h) for jq in range(nq) for h in range(nh)]
    hs = lambda h: slice(h * HEAD_DIM, (h + 1) * HEAD_DIM)
    q3 = jnp.stack([q_ref[jq * qb:(jq + 1) * qb, hs(h)] for jq, h in order])
    k3 = jnp.stack([kext[jq * qb:jq * qb + kw, hs(h)] for jq, h in order])
    v3 = jnp.stack([vext[jq * qb:jq * qb + kw, hs(h)] for jq, h in order])
    shape = (nq * nh, qb, kw)
    qi = lax.broadcasted_iota(I32, shape, 1)
    kj = lax.broadcasted_iota(I32, shape, 2)
    jqi = lax.broadcasted_iota(I32, shape, 0) // nh
    kpos = base + jqi * qb + kj - ATT_R
    mask = (jnp.abs(kj - ATT_R - qi) <= ATT_R) & (kpos >= 0) & (kpos < sub_len)
    s = jnp.where(mask, _bmm_nt(q3, k3), NEG)
    m = jnp.max(s, axis=-1, keepdims=True)
    p = jnp.exp(s - m)
    l = jnp.sum(p, axis=-1, keepdims=True)
    o3 = _bmm(p.astype(BF16), v3) / l
    lse3 = jnp.broadcast_to(m + jnp.log(l), (nq * nh, qb, HEAD_DIM))
    for jq in range(nq):
        o_ref[jq * qb:(jq + 1) * qb, :] = jnp.concatenate(
            [o3[jq * nh + h] for h in range(nh)], axis=1).astype(o_ref.dtype)
        l_ref[jq * qb:(jq + 1) * qb, :] = jnp.concatenate([lse3[jq * nh + h] for h in range(nh)], axis=1)


def _attn_pattern(aq, ak, av, dil, batch, seq_len):
    gw = GROUP_W
    sub = seq_len // dil
    q3, k3, v3 = (t.reshape(batch, sub, dil * gw) for t in (aq, ak, av))
    tq = min(1024, sub)
    nq = sub // tq
    hb = tq // ATT_R
    nhb = sub // ATT_R
    main = pl.BlockSpec((None, tq, gw), lambda b, r, i: (b, i, r))
    prev = pl.BlockSpec((None, ATT_R, gw), lambda b, r, i: (b, jnp.maximum(i * hb - 1, 0), r))
    nxt = pl.BlockSpec((None, ATT_R, gw), lambda b, r, i: (b, jnp.minimum((i + 1) * hb, nhb - 1), r))
    o, l = pl.pallas_call(
        functools.partial(_attn_kernel, sub_len=sub),
        grid=(batch, dil, nq),
        in_specs=[main, prev, main, nxt, prev, main, nxt],
        out_specs=[main, main],
        out_shape=[jax.ShapeDtypeStruct((batch, sub, dil * gw), BF16),
                   jax.ShapeDtypeStruct((batch, sub, dil * gw), F32)],
        compiler_params=_cparams(("arbitrary", "arbitrary", "arbitrary")),
        name=f"dilated_attn_d{dil}",
    )(q3, k3, k3, k3, v3, v3, v3)
    return o.reshape(batch * sub, dil * gw), l.reshape(batch * sub, dil * gw)


DN_TILE = 256


def _deltanet_kernel(qf_ref, kf_ref, vf_ref, gf_ref, qb_ref, kb_ref, vb_ref, gb_ref,
                     of_ref, ob_ref, state_ref):
    nbat, t = qf_ref.shape[0], qf_ref.shape[1]
    cl = DN_CHUNK
    nc = t // cl
    nh = N_HEADS_G
    hd = HEAD_DIM
    gw = GROUP_W
    nb = nbat * 2 * nc

    @pl.when(pl.program_id(0) == 0)
    def _():
        state_ref[...] = jnp.zeros_like(state_ref)

    rc = lax.broadcasted_iota(I32, (t, LANES), 0) % cl
    gcs, gts = [], []
    for d, g_ref in ((0, gf_ref), (1, gb_ref)):
        gcl, gtl = [], []
        for bi in range(nbat):
            gc = g_ref[bi]
            s = 1
            while s < cl:
                if d:
                    gc = jnp.where(rc < cl - s, gc + pltpu.roll(gc, t - s, 0), gc)
                else:
                    gc = jnp.where(rc >= s, gc + pltpu.roll(gc, s, 0), gc)
                s *= 2
            gcl.append(gc)
            gtl.append(gc.T)
        gcs.append(gcl)
        gts.append(gtl)

    order = [(bi, d, c) for bi in range(nbat) for d in range(2) for c in range(nc)]
    data = ((qf_ref, kf_ref, vf_ref, gf_ref), (qb_ref, kb_ref, vb_ref, gb_ref))
    rows = lambda c: slice(c * cl, (c + 1) * cl)

    q3 = jnp.stack([data[d][0][bi, rows(c), :] for bi, d, c in order])
    k3 = jnp.stack([data[d][1][bi, rows(c), :] for bi, d, c in order])
    v3 = jnp.stack([data[d][2][bi, rows(c), :] for bi, d, c in order])

    def lane_expand(src_of, col_of):
        blocks = []
        for bi, d, c in order:
            src = src_of(d, bi)[rows(c), :]
            blocks.append(jnp.concatenate(
                [jnp.broadcast_to(src[:, col_of(d, h):col_of(d, h) + 1], (cl, hd)) for h in range(nh)], axis=1))
        return jnp.stack(blocks)

    g_col = lambda d, h: 2 * nh + d * nh + h
    beta3 = lane_expand(lambda d, bi: data[d][3][bi], lambda d, h: d * nh + h)
    gcc3 = lane_expand(lambda d, bi: gcs[d][bi], g_col)
    gcr3 = jnp.stack([jnp.concatenate([gts[d][bi][g_col(d, h):g_col(d, h) + 1, rows(c)] for h in range(nh)], axis=1)
                      for bi, d, c in order])

    ri = lax.broadcasted_iota(I32, (nb, cl, gw), 1)
    ci = lax.broadcasted_iota(I32, (nb, cl, gw), 2) % hd
    isb = (lax.broadcasted_iota(I32, (nb, cl, gw), 0) // nc) % 2 == 1
    r2 = jnp.where(isb, ci, ri)
    c2 = jnp.where(isb, ri, ci)
    incl = r2 >= c2
    strict = r2 > c2
    eye = (ri == ci).astype(F32)

    def block_diag(x):
        n = x.shape[0]
        same_head = (lax.broadcasted_iota(I32, (n, gw, gw), 1) // hd) == (lax.broadcasted_iota(I32, (n, gw, gw), 2) // hd)
        return jnp.where(same_head, jnp.concatenate([x.astype(BF16)] * nh, axis=1), jnp.zeros((), BF16))

    decay = jnp.where(incl, jnp.exp(jnp.where(incl, gcc3 - gcr3, 0.0)), 0.0)
    kb3 = k3 * beta3
    vb3 = v3 * beta3
    qs = q3 * (hd ** -0.5)
    mk = _bmm_nt(jnp.concatenate([kb3, qs], axis=1).astype(BF16), block_diag(k3))
    m = mk[:, 0:cl, :] * jnp.where(strict, decay, 0.0)
    a3 = (mk[:, cl:2 * cl, :] * decay).astype(BF16)
    blk = lambda s: (ri // s) == (ci // s)
    x4 = jnp.where(blk(4), -m, 0.0)
    x4sq = _bmm(x4.astype(BF16), block_diag(x4))
    tm = eye + x4
    tm = tm + _bmm(tm.astype(BF16), block_diag(x4sq))
    for s in (4, 8, 16, 32):
        cm = jnp.where(blk(2 * s) & jnp.logical_not(blk(s)), m, 0.0)
        tc_ = _bmm(cm.astype(BF16), block_diag(tm))
        tm = tm - _bmm(tm.astype(BF16), block_diag(tc_))
    eg = jnp.exp(gcc3)
    tmb = tm.astype(BF16)
    w3 = _bmm(tmb, block_diag(vb3))
    u3 = _bmm(tmb, block_diag(kb3 * eg)).astype(BF16)
    qd3 = (qs * eg).astype(BF16)
    isb1 = (lax.broadcasted_iota(I32, (nb, 1, gw), 0) // nc) % 2 == 1
    gl3 = jnp.where(isb1, gcc3[:, 0:1, :], gcc3[:, cl - 1:cl, :])
    kd3 = k3 * jnp.exp(gl3 - gcc3)
    kdt3 = _bmm_nt(eye.astype(BF16), block_diag(kd3)).astype(BF16)
    egl3 = jnp.exp(gl3)
    uq3 = jnp.concatenate([u3, qd3], axis=1)
    ak3 = jnp.concatenate([a3, kdt3], axis=1)

    def step_bodies(arr, c):
        parts = []
        for bi in range(nbat):
            f0 = bi * 2 * nc + c
            b0 = bi * 2 * nc + nc + (nc - 1 - c)
            parts += [arr[f0:f0 + 1], arr[b0:b0 + 1]]
        return jnp.concatenate(parts, axis=0)

    st = state_ref[...]
    for c in range(nc):
        us = _bmm(step_bodies(uq3, c), block_diag(st))
        v_new = step_bodies(w3, c) - us[:, 0:cl, :]
        av = _bmm(step_bodies(ak3, c), block_diag(v_new))
        o2 = us[:, cl:2 * cl, :] + av[:, 0:cl, :]
        st = st * step_bodies(egl3, c) + av[:, cl:2 * cl, :]
        for bi in range(nbat):
            of_ref[bi, rows(c), :] = o2[2 * bi]
            ob_ref[bi, rows(nc - 1 - c), :] = o2[2 * bi + 1]
    state_ref[...] = st


def _deltanet(dq, dk, dv, dbg, batch, seq_len):
    n, gw = dq.shape
    t = min(DN_TILE, seq_len)
    nt = seq_len // t
    fwd = lambda i: (0, i, 0)
    bwd = lambda i: (0, nt - 1 - i, 0)
    tf, tb = pl.BlockSpec((batch, t, gw), fwd), pl.BlockSpec((batch, t, gw), bwd)
    gf, gb = pl.BlockSpec((batch, t, LANES), fwd), pl.BlockSpec((batch, t, LANES), bwd)
    dq, dk, dv = (a.reshape(batch, seq_len, gw) for a in (dq, dk, dv))
    dbg = dbg.reshape(batch, seq_len, LANES)
    outs = pl.pallas_call(
        _deltanet_kernel,
        grid=(nt,),
        in_specs=[tf, tf, tf, gf, tb, tb, tb, gb],
        out_specs=[tf, tb],
        out_shape=[jax.ShapeDtypeStruct((batch, seq_len, gw), F32)] * 2,
        scratch_shapes=[pltpu.VMEM((2 * batch, HEAD_DIM, GROUP_W), F32)],
        compiler_params=_cparams(("arbitrary",)),
        name="deltanet",
    )(dq, dk, dv, dbg, dq, dk, dv, dbg)
    return [o.reshape(n, gw) for o in outs]


def _outproj_kernel(x_ref, fy_ref, yb_ref, o1_ref, l1_ref, o2_ref, l2_ref, o3_ref, l3_ref,
                    dof_ref, dob_ref, dsg_ref, wo_ref, nw_ref, g_ref, b_ref, rwt_ref,
                    x1_ref, aff_ref, scr_ref, *, alpha):
    gw = GROUP_W
    tm = x_ref.shape[0]
    att = []
    for p, ((_, dil), refs) in enumerate(zip(DIL_PAIRS, ((o1_ref, l1_ref), (o2_ref, l2_ref), (o3_ref, l3_ref)))):
        for j, ref in enumerate(refs):
            if dil == 1:
                att.append(ref[...].astype(F32))
            else:
                nl = gw // LANES
                for r in range(dil):
                    for c in range(nl):
                        c0 = r * gw + c * LANES
                        scr_ref[2 * p + j, c, pl.ds(r, tm // dil, stride=dil), :] = ref[:, c0:c0 + LANES].astype(F32)
                att.append(jnp.concatenate([scr_ref[2 * p + j, c] for c in range(nl)], axis=1))
    o1, l1, o2, l2, o3, l3 = att
    m = jnp.maximum(jnp.maximum(l1, l2), l3)
    w1, w2, w3 = jnp.exp(l1 - m), jnp.exp(l2 - m), jnp.exp(l3 - m)
    yc = (w1 * o1 + w2 * o2 + w3 * o3) / (w1 + w2 + w3)
    od = dof_ref[...] + dob_ref[...]
    ms = _head_sum(od * od) * (1.0 / HEAD_DIM)
    yd = od * lax.rsqrt(ms + RMS_EPS) * nw_ref[...] * dsg_ref[...]
    y = (_dot(fy_ref[...].astype(BF16), wo_ref[0:gw, :])
         + _dot(yb_ref[...].astype(BF16), wo_ref[gw:2 * gw, :])
         + _dot(yc.astype(BF16), wo_ref[2 * gw:3 * gw, :])
         + _dot(yd.astype(BF16), wo_ref[3 * gw:4 * gw, :]))
    x1 = _layer_norm(alpha * x_ref[...] + y, g_ref[...], b_ref[...])
    x1_ref[...] = x1
    logits = _dot_nt(rwt_ref[...], x1.astype(BF16))
    mx = jnp.max(logits, axis=0, keepdims=True)
    e = jnp.exp(logits - mx)
    aff_ref[...] = e / jnp.sum(e, axis=0, keepdims=True)


def _outproj(x2d, parts, wo, nw, g, b, rwt, batch, seq_len, alpha):
    n, d = x2d.shape
    gw = GROUP_W
    tm = min(512, seq_len)
    tps = seq_len // tm
    row = lambda w: pl.BlockSpec((tm, w), lambda i: (i, 0))
    full = lambda arr: pl.BlockSpec(arr.shape, lambda i: (0,) * arr.ndim)
    att_specs = []
    for _, dil in DIL_PAIRS:
        att_specs += [pl.BlockSpec((tm // dil, dil * gw), lambda i: (i, 0))] * 2
    return pl.pallas_call(
        functools.partial(_outproj_kernel, alpha=alpha),
        grid=(n // tm,),
        in_specs=[row(d)] + [row(gw)] * 2 + att_specs + [row(gw)] * 3
                 + [full(wo), full(nw), full(g), full(b), full(rwt)],
        out_specs=[row(d), pl.BlockSpec((None, N_EXPERTS, tm), lambda i: (i // tps, 0, i % tps))],
        out_shape=[jax.ShapeDtypeStruct((n, d), F32),
                   jax.ShapeDtypeStruct((batch, N_EXPERTS, seq_len), F32)],
        scratch_shapes=[pltpu.VMEM((2 * len(DIL_PAIRS), gw // LANES, tm, LANES), F32)],
        compiler_params=_cparams(("arbitrary",)),
        name="outproj_ln_router",
    )(x2d, *parts, wo, nw, g, b, rwt)


def _topk_kernel(a_ref, idx_ref, gate_ref, slot_ref, off_ref, *, cap):
    ng = a_ref.shape[0]
    v = a_ref[...]
    bits = pltpu.bitcast(v, I32)

    thr = jnp.int32(0)
    for shift in range(28, -1, -4):
        n_cand = 7 if shift == 28 else 15
        digit = jnp.int32(0)
        for c in range(1, n_cand + 1):
            cnt = jnp.sum((bits >= (thr | jnp.int32(c << shift))).astype(I32))
            digit = digit + (cnt >= cap).astype(I32)
        thr = thr | (digit << shift)
    gt = bits > thr
    eq = bits == thr
    need_eq = cap - jnp.sum(gt.astype(I32))

    li = lax.broadcasted_iota(I32, (LANES, LANES), 0)
    lj = lax.broadcasted_iota(I32, (LANES, LANES), 1)
    ut_incl = (li <= lj).astype(BF16)
    gi = lax.broadcasted_iota(I32, (ng, ng), 0)
    gj = lax.broadcasted_iota(I32, (ng, ng), 1)
    lt_strict = (gj < gi).astype(BF16)

    def prefix(mask):
        p1 = _dot(mask.astype(BF16), ut_incl)
        tot = jnp.broadcast_to(p1[:, LANES - 1:LANES], (ng, LANES))
        return p1, _dot(lt_strict, tot.astype(BF16)), tot

    p1e, offe, _ = prefix(eq)
    rank_eq = p1e - eq.astype(F32) + offe
    sel = gt | (eq & (rank_eq < need_eq.astype(F32)))
    p1, offs, tot = prefix(sel)
    slot_ref[...] = jnp.where(sel, (p1 + offs).astype(I32) - 1, -1)
    off_ref[...] = offs.astype(I32)

    j = lax.broadcasted_iota(I32, (cap, 1), 0).astype(F32)
    ends_row = (offs + tot).T[0:1, :]
    offs_row = offs.T[0:1, :]
    gj_ = jnp.sum((ends_row <= j).astype(I32), axis=-1, keepdims=True)
    oh = lax.broadcasted_iota(I32, (cap, ng), 1) == gj_
    off_j = jnp.sum(jnp.where(oh, offs_row, 0.0), axis=-1, keepdims=True)
    ohb = oh.astype(BF16)
    prow = _dot(ohb, p1.astype(BF16))
    lo = jnp.sum((prow <= (j - off_j)).astype(I32), axis=-1, keepdims=True)
    idx_ref[...] = gj_ * LANES + lo
    h1 = v.astype(BF16)
    r1 = v - h1.astype(F32)
    h2 = r1.astype(BF16)
    h3 = (r1 - h2.astype(F32)).astype(BF16)
    arow = _dot(ohb, h1) + _dot(ohb, h2) + _dot(ohb, h3)
    lane = lax.broadcasted_iota(I32, (cap, LANES), 1)
    gate_ref[...] = jnp.sum(jnp.where(lane == lo, arow, 0.0), axis=-1, keepdims=True)


def _topk(aff_t, cap):
    batch, ne, seq_len = aff_t.shape
    ng = seq_len // LANES
    a4 = aff_t.reshape(batch, ne, ng, LANES)
    grp = pl.BlockSpec((None, None, ng, LANES), lambda b, e: (b, e, 0, 0))
    col = pl.BlockSpec((None, None, cap, 1), lambda b, e: (b, e, 0, 0))
    return pl.pallas_call(
        functools.partial(_topk_kernel, cap=cap),
        grid=(batch, ne),
        in_specs=[grp],
        out_specs=[col, col, grp, grp],
        out_shape=[jax.ShapeDtypeStruct((batch, ne, cap, 1), I32),
                   jax.ShapeDtypeStruct((batch, ne, cap, 1), F32),
                   jax.ShapeDtypeStruct((batch, ne, ng, LANES), I32),
                   jax.ShapeDtypeStruct((batch, ne, ng, LANES), I32)],
        compiler_params=_cparams(("arbitrary", "arbitrary")),
        name="expert_topk",
    )(a4)


FFN_ROWS = 512


def _ffn_kernel(idx_ref, idxn_ref, x_hbm, gate_ref, w1_ref, w3_ref, w2_ref, y_hbm,
                xg_ref, gbuf_ref, acc_ref, sem, osem, *, seq_len):
    b = pl.program_id(0)
    e = pl.program_id(1)
    f = pl.program_id(2)
    ne = pl.num_programs(1)
    blk_id = b * ne + e
    n_blocks = pl.num_programs(0) * ne
    cap = xg_ref.shape[0]
    rb = min(FFN_ROWS, cap)

    def row_copy(iref, base, r):
        return pltpu.make_async_copy(x_hbm.at[pl.ds(base + iref[0, 0, r], 1)], gbuf_ref.at[pl.ds(r, 1)], sem)

    def gather_start(iref, base):
        def body(r, c):
            row_copy(iref, base, r).start()
            return c
        lax.fori_loop(0, cap, body, 0, unroll=8)

    @pl.when(f == 0)
    def _():
        @pl.when(blk_id == 0)
        def _():
            gather_start(idx_ref, b * seq_len)

        def wait_body(r, c):
            row_copy(idx_ref, b * seq_len, r).wait()
            return c
        lax.fori_loop(0, cap, wait_body, 0, unroll=8)
        for blk in range(cap // rb):
            rs = slice(blk * rb, (blk + 1) * rb)
            xg_ref[rs, :] = gbuf_ref[rs, :].astype(BF16)
            acc_ref[rs, :] = jnp.zeros((rb, acc_ref.shape[1]), F32)

        @pl.when(blk_id + 1 < n_blocks)
        def _():
            gather_start(idxn_ref, ((blk_id + 1) // ne) * seq_len)

    fc = w1_ref.shape[1]
    w13 = jnp.concatenate([w1_ref[...].astype(BF16), w3_ref[...].astype(BF16)], axis=1)
    w2 = w2_ref[...].astype(BF16)
    for blk in range(cap // rb):
        rs = slice(blk * rb, (blk + 1) * rb)
        h13 = _dot(xg_ref[rs, :], w13)
        h = (jax.nn.silu(h13[:, 0:fc]) * h13[:, fc:2 * fc]).astype(BF16)
        acc_ref[rs, :] += _dot(h, w2)

    @pl.when(f == pl.num_programs(2) - 1)
    def _():
        for blk in range(cap // rb):
            rs = slice(blk * rb, (blk + 1) * rb)
            xg_ref[rs, :] = (acc_ref[rs, :] * gate_ref[rs, :]).astype(BF16)
        out_copy = pltpu.make_async_copy(xg_ref, y_hbm.at[b, e], osem)
        out_copy.start()
        out_copy.wait()


def _ffn(x1, idx, gate, w1, w3, w2, layer, batch, seq_len):
    n, d = x1.shape
    ne, cap = idx.shape[1], idx.shape[2]
    dff = w1.shape[-1]
    fc = 896 if dff % 896 == 0 else dff
    nf = dff // fc
    nblk = batch * ne
    idx3 = idx.reshape(nblk, 1, cap)
    return pl.pallas_call(
        functools.partial(_ffn_kernel, seq_len=seq_len),
        grid=(batch, ne, nf),
        in_specs=[pl.BlockSpec((1, 1, cap), lambda b, e, f: (b * ne + e, 0, 0), memory_space=pltpu.SMEM),
                  pl.BlockSpec((1, 1, cap), lambda b, e, f: (jnp.minimum(b * ne + e + 1, nblk - 1), 0, 0),
                               memory_space=pltpu.SMEM),
                  pl.BlockSpec(memory_space=pl.ANY),
                  pl.BlockSpec((None, None, cap, 1), lambda b, e, f: (b, e, 0, 0)),
                  pl.BlockSpec((None, None, d, fc), lambda b, e, f: (layer, e, 0, f)),
                  pl.BlockSpec((None, None, d, fc), lambda b, e, f: (layer, e, 0, f)),
                  pl.BlockSpec((None, None, fc, d), lambda b, e, f: (layer, e, f, 0))],
        out_specs=pl.BlockSpec(memory_space=pl.ANY),
        out_shape=jax.ShapeDtypeStruct((batch, ne, cap, d), BF16),
        scratch_shapes=[pltpu.VMEM((cap, d), BF16),
                        pltpu.VMEM((cap, d), F32),
                        pltpu.VMEM((cap, d), F32),
                        pltpu.SemaphoreType.DMA(()),
                        pltpu.SemaphoreType.DMA(())],
        compiler_params=_cparams(("arbitrary",) * 3),
        name="expert_ffn",
    )(idx3, idx3, x1, gate, w1, w3, w2)


CMB_WIN = 64
CMB_ALIGN = 16


def _combine_kernel(offs_ref, x_ref, slot_ref, y_hbm, g_ref, b_ref, o_ref, buf_ref, buf2_ref, acc_ref,
                    sem, sem2, *, alpha, cap, tiles_per_seq):
    i = pl.program_id(0)
    nsteps = pl.num_programs(0)
    tc = x_ref.shape[0]
    ne = N_EXPERTS
    kw = ne * CMB_WIN

    def tile(step):
        bidx = step // tiles_per_seq
        return bidx, (bidx * (tiles_per_seq + 1) + step % tiles_per_seq) * ne

    def window(base, e, k):
        start = (offs_ref[base + e] // CMB_ALIGN) * CMB_ALIGN + k * CMB_WIN
        return pl.multiple_of(jnp.minimum(start, cap - CMB_WIN), CMB_ALIGN)

    def first_copy(step, buf, e):
        bidx, base = tile(step)
        return pltpu.make_async_copy(y_hbm.at[bidx, e, pl.ds(window(base, e, 0), CMB_WIN)],
                                     buf_ref.at[buf, pl.ds(e * CMB_WIN, CMB_WIN)], sem.at[buf, e])

    cur = i % 2

    @pl.when(i == 0)
    def _():
        for e in range(ne):
            first_copy(i, cur, e).start()

    @pl.when(i + 1 < nsteps)
    def _():
        for e in range(ne):
            first_copy(i + 1, 1 - cur, e).start()

    bidx, base = tile(i)
    slots = slot_ref[...]
    sp = slots + 1
    el = lax.broadcasted_iota(I32, (ne, kw), 1) // CMB_WIN
    expand = (el == lax.broadcasted_iota(I32, (ne, kw), 0)).astype(BF16)
    sp_exp = (_dot((sp >> 6).astype(F32).astype(BF16), expand) * 64.0
              + _dot((sp & 63).astype(F32).astype(BF16), expand))
    lane1 = lax.broadcasted_iota(I32, (1, kw), 1)
    win_row = jnp.zeros((1, kw), I32)
    for e in range(ne):
        win_row = jnp.where(lane1 // CMB_WIN == e, window(base, e, 0) + 1, win_row)
    oh = (sp_exp - win_row.astype(F32)) == (lane1 % CMB_WIN).astype(F32)
    for e in range(ne):
        first_copy(i, cur, e).wait()
    acc_ref[...] = alpha * x_ref[...] + _dot(oh.astype(BF16), buf_ref[cur])

    def n_windows(e):
        off = offs_ref[base + e]
        end = offs_ref[base + ne + e]
        return (end - (off // CMB_ALIGN) * CMB_ALIGN + CMB_WIN - 1) // CMB_WIN

    overflow = n_windows(0) > 1
    for e in range(1, ne):
        overflow = overflow | (n_windows(e) > 1)

    @pl.when(overflow)
    def _():
        lane = lax.broadcasted_iota(I32, (tc, CMB_WIN), 1)
        for e in range(ne):
            slot = slots[:, e:e + 1]

            def extra(k, c):
                w0 = window(base, e, k)
                cp = pltpu.make_async_copy(y_hbm.at[bidx, e, pl.ds(w0, CMB_WIN)], buf2_ref, sem2)
                cp.start()
                cp.wait()
                ohk = ((slot - w0) == lane) & ((slot - window(base, e, 0)) >= k * CMB_WIN)
                acc_ref[...] += _dot(ohk.astype(BF16), buf2_ref[...])
                return c

            lax.fori_loop(1, jnp.maximum(n_windows(e), 1), extra, 0)

    o_ref[...] = _layer_norm(acc_ref[...], g_ref[...], b_ref[...])


def _combine(x1, y, slot_tm, offs, g, b, batch, seq_len, alpha):
    n, d = x1.shape
    ne, cap = y.shape[1], y.shape[2]
    tc = min(256, seq_len)
    tps = seq_len // tc
    return pl.pallas_call(
        functools.partial(_combine_kernel, alpha=alpha, cap=cap, tiles_per_seq=tps),
        grid_spec=pltpu.PrefetchScalarGridSpec(
            num_scalar_prefetch=1,
            grid=(n // tc,),
            in_specs=[pl.BlockSpec((tc, d), lambda i, o: (i, 0)),
                      pl.BlockSpec((tc, ne), lambda i, o: (i, 0)),
                      pl.BlockSpec(memory_space=pl.ANY),
                      pl.BlockSpec((1, d), lambda i, o: (0, 0)),
                      pl.BlockSpec((1, d), lambda i, o: (0, 0))],
            out_specs=pl.BlockSpec((tc, d), lambda i, o: (i, 0)),
            scratch_shapes=[pltpu.VMEM((2, ne * CMB_WIN, d), BF16),
                            pltpu.VMEM((CMB_WIN, d), BF16),
                            pltpu.VMEM((tc, d), F32),
                            pltpu.SemaphoreType.DMA((2, ne)),
                            pltpu.SemaphoreType.DMA(())]),
        out_shape=jax.ShapeDtypeStruct((n, d), F32),
        compiler_params=_cparams(("arbitrary",)),
        name="combine_ln",
    )(offs, x1, slot_tm, y, g, b)


def _moe(x1, aff_t, w1, w3, w2, g2, b2, layer, batch, seq_len, alpha):
    cap = EC_FACTOR * seq_len // N_EXPERTS
    idx, gate, slot, offs = _topk(aff_t, cap)
    y = _ffn(x1, idx[..., 0], gate, w1, w3, w2, layer, batch, seq_len)
    tc = min(256, seq_len)
    slot_tm = jnp.transpose(slot.reshape(batch, N_EXPERTS, seq_len), (0, 2, 1)).reshape(batch * seq_len, N_EXPERTS)
    tile_off = offs[:, :, ::tc // LANES, 0]
    tile_off = jnp.concatenate([tile_off, jnp.full((batch, N_EXPERTS, 1), cap, I32)], axis=2)
    tile_off = jnp.transpose(tile_off, (0, 2, 1)).reshape(-1)
    return _combine(x1, y, slot_tm, tile_off, g2, b2, batch, seq_len, alpha)


def kernel(x, positions, w_in, w_out, fno_w, fno_b, lru_conv_w, lru_conv_b, lru_wa, lru_ba, lru_wx, lru_bx,
           lru_lam, dn_conv_w, dn_conv_b, dn_a_log, dn_dt_bias, dn_norm_w, ln1_g, ln1_b, router_w,
           exp_w1, exp_w3, exp_w2, ln2_g, ln2_b):
    batch, seq_len, d = x.shape
    depth = w_in.shape[0]
    gw = GROUP_W
    nh2 = 2 * N_HEADS_G
    alpha = (2.0 * depth) ** 0.25
    n = batch * seq_len

    cos_t, sin_t = _rope_tables(positions)
    tables = _dft_tables(seq_len)

    col = lambda a, b: w_in[:, :, a * gw:b * gw]
    bd = jnp.pad(w_in[:, :, 10 * gw:], ((0, 0), (0, 0), (0, LANES - 2 * nh2)))
    wm = jnp.concatenate([col(0, 1), col(2, 3), col(3, 6), col(9, 10), bd], axis=2).astype(BF16)
    wc = jnp.concatenate([col(1, 2), col(6, 9)], axis=2).astype(BF16)
    cw = jnp.concatenate([lru_conv_w, dn_conv_w], axis=2)
    cb = jnp.concatenate([lru_conv_b, dn_conv_b], axis=1)[:, None, :]
    pad16 = lambda v: jnp.pad(v.reshape(depth, 1, nh2), ((0, 0), (0, 0), (nh2, LANES - 2 * nh2)))
    nalog = pad16(-jnp.exp(dn_a_log.astype(F32)))
    dtb = pad16(dn_dt_bias)
    wf_bd = jax.vmap(_block_diag)(fno_w).astype(BF16)
    bf = fno_b.reshape(depth, 1, gw)
    lru_w = jnp.concatenate([jax.vmap(jax.vmap(_block_diag))(lru_wa),
                             jax.vmap(jax.vmap(_block_diag))(lru_wx)], axis=3).astype(BF16)
    lru_b = jnp.concatenate([lru_ba, lru_bx], axis=2)[:, :, None, :]
    lam = lru_lam[:, :, None, :]
    wo = w_out.astype(BF16)
    nw = jnp.tile(dn_norm_w, (1, N_HEADS_G))[:, None, :]
    rwt = jnp.transpose(router_w, (0, 2, 1)).astype(BF16)

    x2d = x.reshape(n, d)
    for l in range(depth):
        outs = _inproj(x2d, cos_t, sin_t, wm[l], wc[l], cw[l], cb[l], nalog[l], dtb[l], seq_len)
        fa, xc, gg = outs[:3]
        n_att = 3 * len(DIL_PAIRS)
        att_in = outs[3:3 + n_att]
        dq, dk, dv, dsg, dbg = outs[3 + n_att:]
        fy = _fourier(fa, tables, wf_bd[l], bf[l], batch, seq_len)
        yb = _lru(xc, gg, lru_w[l], lru_b[l], lam[l], batch, seq_len)
        att = []
        for p, (window, dil) in enumerate(DIL_PAIRS):
            assert (window // 2) // dil == ATT_R
            att += list(_attn_pattern(*att_in[3 * p:3 * p + 3], dil, batch, seq_len))
        dof, dob = _deltanet(dq, dk, dv, dbg, batch, seq_len)
        x1, aff_t = _outproj(x2d, [fy, yb] + att + [dof, dob, dsg], wo[l], nw[l],
                             ln1_g[l][None, :], ln1_b[l][None, :], rwt[l], batch, seq_len, alpha)
        x2d = _moe(x1, aff_t, exp_w1, exp_w3, exp_w2, ln2_g[l][None, :], ln2_b[l][None, :],
                   l, batch, seq_len, alpha)
    return x2d.reshape(batch, seq_len, d)
```
